```python
import math
import jax, jax.numpy as jnp
from jax import lax
import numpy as np

D_MODEL = 1024
BATCH = 8
SEQ = 4096
DEPTH = 2
DEC_BATCH = 8
DEC_SEQ = 2048
PAST_LEN = 128

HA = 8
HKV = 2
GA = HA // HKV
DH = 64
WINDOW = 128
BLK = 128
HB = 4
DVB = 2 * DH
QA_W = HA * DH
KA_W = HKV * DH
VA_W = HKV * DH
QB_W = HB * 2 * DH
KB_W = HB * 2 * DH
VB_W = HB * DVB
WIDTH_A = QA_W
WIDTH_B = VB_W
IN_W = QA_W + KA_W + VA_W + QB_W + KB_W + VB_W + 2 * D_MODEL
NUM_BUCKETS = 32
MAX_DISTANCE = 128
N_BIAS_HEADS = HA + HB
N_KEYS = 128
N_EXPERTS = N_KEYS * N_KEYS
PEER_HEADS = 8
PEER_DQ = 256
PEER_DHALF = PEER_DQ // 2
PEER_TOPK = 16
PEER_CHUNK = 128
PLE_DIM = 256
ALPHA = (2.0 * DEPTH) ** 0.25
BETA = (8.0 * DEPTH) ** -0.25
LN_EPS = 1e-5
NEG_INF = -1e30

kernel_name = 'hybrid_window_diff_peer_encoder'


def layer_norm(x, g, b):
    xf = x.astype(jnp.float32)
    mu = jnp.mean(xf, axis=-1, keepdims=True)
    var = jnp.mean(jnp.square(xf - mu), axis=-1, keepdims=True)
    y = (xf - mu) * lax.rsqrt(var + LN_EPS) * g.astype(jnp.float32) + b.astype(jnp.float32)
    return y.astype(x.dtype)


def t5_bucket(rel):
    half = NUM_BUCKETS // 2
    max_exact = half // 2
    ret = jnp.where(rel > 0, half, 0)
    n = jnp.abs(rel)
    nf = jnp.maximum(n, 1).astype(jnp.float32)
    large = max_exact + (jnp.log(nf / max_exact) / math.log(MAX_DISTANCE / max_exact)
                         * (half - max_exact)).astype(jnp.int32)
    large = jnp.minimum(large, half - 1)
    return ret + jnp.where(n < max_exact, n, large)


def window_gqa(q, k, v, table_a, sink):
    B, S = q.shape[0], q.shape[1]
    nb = S // BLK
    qb = q.reshape(B, nb, BLK, HKV, GA, DH)
    pad = ((0, 0), (WINDOW, WINDOW), (0, 0), (0, 0))
    kp = jnp.pad(k, pad)
    vp = jnp.pad(v, pad)

    def blocks(t):
        return jnp.concatenate([t[:, o:o + S].reshape(B, nb, BLK, HKV, DH) for o in (0, BLK, 2 * BLK)], axis=2)

    kb = blocks(kp)
    vb = blocks(vp)
    s = jnp.einsum('bnqhgd,bnkhd->bnhgqk', qb, kb).astype(jnp.float32) * (DH ** -0.5)
    i = jnp.arange(BLK, dtype=jnp.int32)[:, None]
    j = jnp.arange(3 * BLK, dtype=jnp.int32)[None, :]
    rel = j - BLK - i
    bias = table_a.astype(jnp.float32)[t5_bucket(rel)]
    bias = jnp.transpose(bias, (2, 0, 1)).reshape(HKV, GA, BLK, 3 * BLK)
    kpos = (jnp.arange(nb, dtype=jnp.int32)[:, None, None] * BLK - BLK + j[None])
    valid = (jnp.abs(rel) <= WINDOW)[None] & (kpos >= 0) & (kpos < S)
    s = jnp.where(valid[None, :, None, None], s + bias, NEG_INF)
    sink_col = jnp.broadcast_to(sink.astype(jnp.float32).reshape(HKV, GA, 1, 1), s.shape[:-1] + (1,))
    pr = jax.nn.softmax(jnp.concatenate([s, sink_col], axis=-1), axis=-1)[..., :-1]
    out = jnp.einsum('bnhgqk,bnkhd->bnqhgd', pr.astype(v.dtype), vb)
    return out.reshape(B, S, WIDTH_A)


def diff_attention(q, k, v, table_b, lam, lam_init, norm_g):
    B, S = q.shape[0], q.shape[1]
    nb = S // BLK
    qb = jnp.moveaxis(q.reshape(B, nb, BLK, HB, 2, DH), 1, 0)
    kpos = jnp.arange(S, dtype=jnp.int32)
    tb = table_b.astype(jnp.float32)

    def one_block(args):
        qblk, n = args
        s = jnp.einsum('bqhcd,bkhcd->bhcqk', qblk, k).astype(jnp.float32) * (DH ** -0.5)
        qpos = n * BLK + jnp.arange(BLK, dtype=jnp.int32)
        bias = tb[t5_bucket(kpos[None, :] - qpos[:, None])]
        s = s + jnp.transpose(bias, (2, 0, 1))[None, :, None]
        pr = jax.nn.softmax(s, axis=-1)
        a = pr[:, :, 0] - lam * pr[:, :, 1]
        return jnp.einsum('bhqk,bkhe->bqhe', a.astype(v.dtype), v)

    o = lax.map(one_block, (qb, jnp.arange(nb, dtype=jnp.int32)))
    o = jnp.moveaxis(o, 0, 1).reshape(B, S, HB, DVB).astype(jnp.float32)
    o = o * lax.rsqrt(jnp.mean(jnp.square(o), axis=-1, keepdims=True) + LN_EPS)
    o = o * norm_g.astype(jnp.float32) * (1.0 - lam_init)
    return o.astype(v.dtype).reshape(B, S, WIDTH_B)


def peer(x, wq, subkeys, u_tab, v_tab):
    B, S, D = x.shape
    xc = x.reshape(B * S // PEER_CHUNK, PEER_CHUNK, D)

    def one_chunk(xt):
        q = (xt @ wq).reshape(PEER_CHUNK, PEER_HEADS, 2, PEER_DHALF)
        sc = jnp.einsum('chpd,hpkd->chpk', q, subkeys)
        s1, i1 = lax.top_k(sc[:, :, 0], PEER_TOPK)
        s2, i2 = lax.top_k(sc[:, :, 1], PEER_TOPK)
        cand = (s1[..., :, None] + s2[..., None, :]).reshape(PEER_CHUNK, PEER_HEADS, PEER_TOPK * PEER_TOPK)
        cidx = (i1[..., :, None] * N_KEYS + i2[..., None, :]).reshape(PEER_CHUNK, PEER_HEADS, PEER_TOPK * PEER_TOPK)
        top, pos = lax.top_k(cand, PEER_TOPK)
        eidx = jnp.take_along_axis(cidx, pos, axis=-1)
        g = jax.nn.softmax(top.astype(jnp.float32), axis=-1)
        h = jax.nn.gelu(jnp.einsum('cd,chkd->chk', xt, u_tab[eidx]).astype(jnp.float32), approximate=False)
        w = (g * h).astype(xt.dtype)
        return jnp.einsum('chk,chkd->cd', w, v_tab[eidx])

    return lax.map(one_chunk, xc).reshape(B, S, D)


def trunk(x, p, emb_ln_g, emb_ln_b, rel_bias, w_in, w_a, w_b, w_o, sink,
          lam_q1, lam_k1, lam_q2, lam_k2, diff_norm_g, ln1_g, ln1_b,
          peer_wq, peer_subkeys, peer_u, peer_v, ln2_g, ln2_b, ple_w, ple_gate_w):
    B, S, D = x.shape
    x = layer_norm(x, emb_ln_g, emb_ln_b)
    table_a = rel_bias[:, :HA]
    table_b = rel_bias[:, HA:]
    for i in range(DEPTH):
        z = x @ w_in[i]
        o = 0
        qa = z[..., o:o + QA_W].reshape(B, S, HA, DH); o += QA_W
        ka = z[..., o:o + KA_W].reshape(B, S, HKV, DH); o += KA_W
        va = z[..., o:o + VA_W].reshape(B, S, HKV, DH); o += VA_W
        qb = z[..., o:o + QB_W].reshape(B, S, HB, 2, DH); o += QB_W
        kb = z[..., o:o + KB_W].reshape(B, S, HB, 2, DH); o += KB_W
        vb = z[..., o:o + VB_W].reshape(B, S, HB, DVB); o += VB_W
        gate_a = jax.nn.sigmoid(z[..., o:o + D]); o += D
        gate_b = jax.nn.sigmoid(z[..., o:o + D])

        a = window_gqa(qa, ka, va, table_a, sink[i]) @ w_a[i]
        lam_init = 0.8 - 0.6 * math.exp(-0.3 * i)
        lam = (jnp.exp(jnp.sum(lam_q1[i].astype(jnp.float32) * lam_k1[i].astype(jnp.float32)))
               - jnp.exp(jnp.sum(lam_q2[i].astype(jnp.float32) * lam_k2[i].astype(jnp.float32)))
               + lam_init)
        b = diff_attention(qb, kb, vb, table_b, lam, lam_init, diff_norm_g[i]) @ w_b[i]
        mix = (gate_a * a + gate_b * b) @ w_o[i]
        x = layer_norm(ALPHA * x + mix, ln1_g[i], ln1_b[i])

        ch = peer(x, peer_wq[i], peer_subkeys[i], peer_u[i], peer_v[i])
        ple = jax.nn.sigmoid(x @ ple_gate_w[i]) * (p[i] @ ple_w[i])
        x = layer_norm(ALPHA * x + ch + ple, ln2_g[i], ln2_b[i])
    return x


def setup_inputs(seed: int = 0) -> dict:
    key = jax.random.key(seed)
    ks = jax.random.split(key, 32)

    def nrm(k, shape, scale):
        return jax.random.normal(k, shape, jnp.float32) * scale

    D = D_MODEL
    return {
        'x_prompt': nrm(ks[0], (BATCH, SEQ, D), 1.0),
        'x_sample': nrm(ks[1], (DEC_BATCH, DEC_SEQ, D), 1.0),
        'p_prompt': nrm(ks[2], (DEPTH, BATCH, SEQ, PLE_DIM), 1.0),
        'p_sample': nrm(ks[3], (DEPTH, DEC_BATCH, DEC_SEQ, PLE_DIM), 1.0),
        'emb_ln_g': 1.0 + nrm(ks[4], (D,), 0.02),
        'emb_ln_b': nrm(ks[5], (D,), 0.02),
        'rel_bias': nrm(ks[6], (NUM_BUCKETS, N_BIAS_HEADS), 0.5),
        'w_in': nrm(ks[7], (DEPTH, D, IN_W), D ** -0.5),
        'w_a': nrm(ks[8], (DEPTH, WIDTH_A, D), WIDTH_A ** -0.5),
        'w_b': nrm(ks[9], (DEPTH, WIDTH_B, D), WIDTH_B ** -0.5),
        'w_o': nrm(ks[10], (DEPTH, D, D), BETA * D ** -0.5),
        'sink': nrm(ks[11], (DEPTH, HA), 0.5),
        'lam_q1': nrm(ks[12], (DEPTH, DH), 0.1),
        'lam_k1': nrm(ks[13], (DEPTH, DH), 0.1),
        'lam_q2': nrm(ks[14], (DEPTH, DH), 0.1),
        'lam_k2': nrm(ks[15], (DEPTH, DH), 0.1),
        'diff_norm_g': 1.0 + nrm(ks[16], (DEPTH, DVB), 0.02),
        'ln1_g': 1.0 + nrm(ks[17], (DEPTH, D), 0.02),
        'ln1_b': nrm(ks[18], (DEPTH, D), 0.02),
        'peer_wq': nrm(ks[19], (DEPTH, D, PEER_HEADS * PEER_DQ), D ** -0.5),
        'peer_subkeys': nrm(ks[20], (DEPTH, PEER_HEADS, 2, N_KEYS, PEER_DHALF), PEER_DHALF ** -0.5),
        'peer_u': nrm(ks[21], (DEPTH, N_EXPERTS, D), D ** -0.5),
        'peer_v': nrm(ks[22], (DEPTH, N_EXPERTS, D), BETA),
        'ln2_g': 1.0 + nrm(ks[23], (DEPTH, D), 0.02),
        'ln2_b': nrm(ks[24], (DEPTH, D), 0.02),
        'ple_w': nrm(ks[25], (DEPTH, PLE_DIM, D), PLE_DIM ** -0.5),
        'ple_gate_w': nrm(ks[26], (DEPTH, D, D), D ** -0.5),
    }


def reference(x_prompt, x_sample, p_prompt, p_sample, emb_ln_g, emb_ln_b, rel_bias, w_in, w_a, w_b, w_o,
              sink, lam_q1, lam_k1, lam_q2, lam_k2, diff_norm_g, ln1_g, ln1_b,
              peer_wq, peer_subkeys, peer_u, peer_v, ln2_g, ln2_b, ple_w, ple_gate_w):
    y_prompt = trunk(x_prompt, p_prompt, emb_ln_g, emb_ln_b, rel_bias, w_in, w_a, w_b, w_o, sink,
                     lam_q1, lam_k1, lam_q2, lam_k2, diff_norm_g, ln1_g, ln1_b,
                     peer_wq, peer_subkeys, peer_u, peer_v, ln2_g, ln2_b, ple_w, ple_gate_w)
    y_sample = trunk(x_sample, p_sample, emb_ln_g, emb_ln_b, rel_bias, w_in, w_a, w_b, w_o, sink,
                     lam_q1, lam_k1, lam_q2, lam_k2, diff_norm_g, ln1_g, ln1_b,
                     peer_wq, peer_subkeys, peer_u, peer_v, ln2_g, ln2_b, ple_w, ple_gate_w)
    return (y_prompt, y_sample)
```

```python
import functools
import math

import numpy as np
import jax
import jax.numpy as jnp
from jax import lax
from jax.experimental import pallas as pl
from jax.experimental.pallas import tpu as pltpu

F32 = jnp.float32
BF16 = jnp.bfloat16

D_MODEL = 1024
DEPTH = 2
HA, HKV, DH = 8, 2, 64
WINDOW = 128
BLK = 128
HB = 4
DVB = 2 * DH
N_BUCKETS = 32
MAX_DIST = 128
N_KEYS = 128
N_EXPERTS = N_KEYS * N_KEYS
PEER_HEADS = 8
PEER_TOPK = 16
PLE_DIM = 256
ALPHA = (2.0 * DEPTH) ** 0.25
LN_EPS = 1e-5
NEG = -1e30

VMEM_LIMIT = 56 * 1024 * 1024

NT_DIMS = (((1,), (1,)), ((), ()))


def _cparams(sem):
    return pltpu.CompilerParams(dimension_semantics=sem, vmem_limit_bytes=VMEM_LIMIT)


def _layer_norm(x, g, b):
    mu = jnp.mean(x, axis=-1, keepdims=True)
    xc = x - mu
    var = jnp.mean(xc * xc, axis=-1, keepdims=True)
    return xc * lax.rsqrt(var + LN_EPS) * g + b


def _ln_kernel(x_ref, g_ref, b_ref, o_ref):
    o_ref[...] = _layer_norm(x_ref[...], g_ref[...], b_ref[...])


def _embed_ln(x, g, b, tm=512):
    t, d = x.shape
    row = pl.BlockSpec((1, d), lambda i: (0, 0))
    return pl.pallas_call(
        _ln_kernel,
        out_shape=jax.ShapeDtypeStruct((t, d), F32),
        grid=(t // tm,),
        in_specs=[pl.BlockSpec((tm, d), lambda i: (i, 0)), row, row],
        out_specs=pl.BlockSpec((tm, d), lambda i: (i, 0)),
        compiler_params=_cparams(("parallel",)),
        name="embed_ln",
    )(x, g.reshape(1, d), b.reshape(1, d))


_QA0, _KA0, _VA0, _QB0, _KB0, _VB0, _GA0, _GB0, _INW = 0, 512, 768, 1024, 1536, 2048, 2560, 3584, 4608


def _inproj_kernel(x_ref, w_ref, qa_ref, ka_ref, va_ref, qb_ref, kb_ref, vb_ref, ga_ref, gb_ref):
    x = x_ref[...].astype(BF16)

    def proj(lo, hi):
        return jnp.dot(x, w_ref[:, lo:hi], preferred_element_type=F32)

    qa_ref[...] = proj(_QA0, _KA0).astype(BF16)
    ka_ref[...] = proj(_KA0, _VA0).astype(BF16)
    va_ref[...] = proj(_VA0, _QB0).astype(BF16)
    qb_ref[...] = proj(_QB0, _KB0).astype(BF16)
    kb_ref[...] = proj(_KB0, _VB0).astype(BF16)
    vb_ref[...] = proj(_VB0, _GA0).astype(BF16)
    ga_ref[...] = jax.nn.sigmoid(proj(_GA0, _GB0)).astype(BF16)
    gb_ref[...] = jax.nn.sigmoid(proj(_GB0, _INW)).astype(BF16)


def _in_proj(x, w, tm=512):
    t, d = x.shape
    widths = (512, 256, 256, 512, 512, 512, 1024, 1024)
    return pl.pallas_call(
        _inproj_kernel,
        out_shape=[jax.ShapeDtypeStruct((t, n), BF16) for n in widths],
        grid=(t // tm,),
        in_specs=[pl.BlockSpec((tm, d), lambda i: (i, 0)), pl.BlockSpec((d, _INW), lambda i: (0, 0))],
        out_specs=[pl.BlockSpec((tm, n), lambda i: (i, 0)) for n in widths],
        compiler_params=_cparams(("parallel",)),
        name="in_proj",
    )(x, w)


def _prep_w_in(w):
    sc = DH ** -0.5
    qa = w[:, 0:512] * sc
    ka = w[:, 512:640]
    va = w[:, 640:768]
    qb = w[:, 768:1280] * sc
    rest = w[:, 1280:]
    dup = lambda m: jnp.concatenate([m[:, 0:64], m[:, 0:64], m[:, 64:128], m[:, 64:128]], axis=1)
    return jnp.concatenate([qa, dup(ka), dup(va), qb, rest], axis=1).astype(BF16)


def _t5_bucket(rel):
    half = N_BUCKETS // 2
    max_exact = half // 2
    ret = jnp.where(rel > 0, half, 0)
    n = jnp.abs(rel)
    nf = jnp.maximum(n, 1).astype(F32)
    large = max_exact + (jnp.log(nf / max_exact) / math.log(MAX_DIST / max_exact)
                         * (half - max_exact)).astype(jnp.int32)
    large = jnp.minimum(large, half - 1)
    return ret + jnp.where(n < max_exact, n, large)


def _window_bias(table_a):
    i = jnp.arange(BLK, dtype=jnp.int32)[:, None]
    j = jnp.arange(3 * BLK, dtype=jnp.int32)[None, :]
    rel = j - BLK - i
    bias = jnp.transpose(table_a.astype(F32)[_t5_bucket(rel)], (2, 0, 1))
    return jnp.where((jnp.abs(rel) <= WINDOW)[None], bias, NEG)


def _diff_bias(table_b, t):
    i = jnp.arange(t, dtype=jnp.int32)[:, None]
    j = jnp.arange(t, dtype=jnp.int32)[None, :]
    tiles = []
    for d in (-2, -1, 0, 1, 2):
        rel = d * t + j - i
        tiles.append(jnp.transpose(table_b.astype(F32)[_t5_bucket(rel)], (2, 0, 1)))
    return jnp.stack(tiles, axis=0)


def _window_kernel(q_ref, kp_ref, kc_ref, kn_ref, vp_ref, vc_ref, vn_ref, bias_ref, sink_ref, o_ref):
    n = pl.program_id(1)
    nb = pl.num_programs(1)
    q = q_ref[0]
    k3 = jnp.concatenate([kp_ref[0], kc_ref[0], kn_ref[0]], axis=0)
    v3 = jnp.concatenate([vp_ref[0], vc_ref[0], vn_ref[0]], axis=0)
    lane = lax.broadcasted_iota(jnp.int32, (1, 128), 1)
    lo = lane < DH
    col = lax.broadcasted_iota(jnp.int32, (1, 3 * BLK), 1)
    off_edge = ((col < BLK) & (n == 0)) | ((col >= 2 * BLK) & (n == nb - 1))
    edge = jnp.where(off_edge, NEG, 0.0).astype(F32)
    zero = jnp.zeros((), BF16)
    for pair in range(HA // 2):
        kvh = (2 * pair) // (HA // HKV)
        q2 = q[:, 128 * pair:128 * (pair + 1)]
        kk = k3[:, 128 * kvh:128 * (kvh + 1)]
        vv = v3[:, 128 * kvh:128 * (kvh + 1)]
        outs = []
        for half in range(2):
            hd = 2 * pair + half
            qm = jnp.where(lo if half == 0 else jnp.logical_not(lo), q2, zero)
            s = lax.dot_general(qm, kk, NT_DIMS, preferred_element_type=F32)
            s = s + bias_ref[hd] + edge
            sk = sink_ref[hd:hd + 1, 0:1]
            m = jnp.maximum(jnp.max(s, axis=-1, keepdims=True), sk)
            p = jnp.exp(s - m)
            denom = jnp.sum(p, axis=-1, keepdims=True) + jnp.exp(sk - m)
            o = jnp.dot(p.astype(BF16), vv, preferred_element_type=F32)
            outs.append(o / denom)
        o_ref[0, :, 128 * pair:128 * (pair + 1)] = jnp.where(lo, outs[0], outs[1]).astype(BF16)


def _window_attn(qa, ka, va, bias_a, sink_b):
    b, s, _ = qa.shape
    nb = s // BLK
    kv = lambda f: pl.BlockSpec((1, BLK, 256), f)
    prev = lambda bi, n: (bi, jnp.maximum(n - 1, 0), 0)
    cur = lambda bi, n: (bi, n, 0)
    nxt = lambda bi, n: (bi, jnp.minimum(n + 1, nb - 1), 0)
    return pl.pallas_call(
        _window_kernel,
        out_shape=jax.ShapeDtypeStruct((b, s, 512), BF16),
        grid=(b, nb),
        in_specs=[pl.BlockSpec((1, BLK, 512), cur), kv(prev), kv(cur), kv(nxt), kv(prev), kv(cur), kv(nxt),
                  pl.BlockSpec((HA, BLK, 3 * BLK), lambda bi, n: (0, 0, 0)),
                  pl.BlockSpec((HA, 128), lambda bi, n: (0, 0))],
        out_specs=pl.BlockSpec((1, BLK, 512), cur),
        compiler_params=_cparams(("parallel", "arbitrary")),
        name="window_attn",
    )(qa, ka, ka, ka, va, va, va, bias_a, sink_b)


def _diff_kernel(q_ref, k_ref, v_ref, bias_ref, lamv_ref, g_ref, o_ref, m_scr, acc_scr, *, lam_init):
    j = pl.program_id(2)
    nk = pl.num_programs(2)
    tq = q_ref.shape[1]
    tk = k_ref.shape[1]

    @pl.when(j == 0)
    def _():
        m_scr[...] = jnp.full(m_scr.shape, -jnp.inf, F32)
        acc_scr[...] = jnp.zeros(acc_scr.shape, F32)

    lane = lax.broadcasted_iota(jnp.int32, (1, 128), 1)
    lo = lane < DH
    zero = jnp.zeros((), BF16)
    ones = jnp.ones((tk, 128), BF16)
    for h in range(HB):
        qh = q_ref[0, :, 128 * h:128 * (h + 1)]
        kh = k_ref[0, :, 128 * h:128 * (h + 1)]
        vaug = jnp.concatenate([v_ref[0, :, 128 * h:128 * (h + 1)], ones], axis=1)
        bias = bias_ref[0, h]
        for c in range(2):
            idx = 2 * h + c
            qm = jnp.where(lo if c == 0 else jnp.logical_not(lo), qh, zero)
            s = lax.dot_general(qm, kh, NT_DIMS, preferred_element_type=F32) + bias
            m_old = m_scr[idx]
            m_new = jnp.maximum(m_old, jnp.max(s, axis=-1, keepdims=True))
            p = jnp.exp(s - m_new)
            acc_scr[idx] = acc_scr[idx] * jnp.exp(m_old - m_new) + jnp.dot(
                p.astype(BF16), vaug, preferred_element_type=F32)
            m_scr[idx] = m_new

    @pl.when(j == nk - 1)
    def _():
        lv = lamv_ref[...]
        dot1 = jnp.sum(lv[0:1] * lv[1:2], axis=-1, keepdims=True)
        dot2 = jnp.sum(lv[2:3] * lv[3:4], axis=-1, keepdims=True)
        lam = jnp.exp(dot1) - jnp.exp(dot2) + lam_init
        for h in range(HB):
            a0 = acc_scr[2 * h]
            a1 = acc_scr[2 * h + 1]
            o = a0[:, :128] / a0[:, 128:] - lam * (a1[:, :128] / a1[:, 128:])
            o = o * lax.rsqrt(jnp.mean(o * o, axis=-1, keepdims=True) + LN_EPS)
            o = o * g_ref[...] * (1.0 - lam_init)
            o_ref[0, :, 128 * h:128 * (h + 1)] = o.astype(BF16)


def _diff_attn(qb, kb, vb, bias5, lamv, norm_g, lam_init, t):
    b, s, _ = qb.shape
    nt = s // t
    return pl.pallas_call(
        functools.partial(_diff_kernel, lam_init=lam_init),
        out_shape=jax.ShapeDtypeStruct((b, s, 512), BF16),
        grid=(b, nt, nt),
        in_specs=[pl.BlockSpec((1, t, 512), lambda bi, i, j: (bi, i, 0)),
                  pl.BlockSpec((1, t, 512), lambda bi, i, j: (bi, j, 0)),
                  pl.BlockSpec((1, t, 512), lambda bi, i, j: (bi, j, 0)),
                  pl.BlockSpec((1, HB, t, t), lambda bi, i, j: (jnp.clip(j - i, -2, 2) + 2, 0, 0, 0)),
                  pl.BlockSpec((4, DH), lambda bi, i, j: (0, 0)),
                  pl.BlockSpec((1, DVB), lambda bi, i, j: (0, 0))],
        out_specs=pl.BlockSpec((1, t, 512), lambda bi, i, j: (bi, i, 0)),
        scratch_shapes=[pltpu.VMEM((2 * HB, t, 1), F32), pltpu.VMEM((2 * HB, t, 256), F32)],
        compiler_params=_cparams(("parallel", "parallel", "arbitrary")),
        name="diff_attn",
    )(qb, kb, vb, bias5, lamv, norm_g)


def _mix_kernel(a_ref, b_ref, ga_ref, gb_ref, x_ref, wa_ref, wb_ref, wo_ref, g_ref, bb_ref, o_ref):
    a = jnp.dot(a_ref[...], wa_ref[...], preferred_element_type=F32)
    b = jnp.dot(b_ref[...], wb_ref[...], preferred_element_type=F32)
    mix_in = ga_ref[...].astype(F32) * a + gb_ref[...].astype(F32) * b
    mix = jnp.dot(mix_in.astype(BF16), wo_ref[...], preferred_element_type=F32)
    o_ref[...] = _layer_norm(ALPHA * x_ref[...] + mix, g_ref[...], bb_ref[...])


def _attn_mix(a, b, ga, gb, x, wa, wb, wo, g, bb, tm=512):
    t, d = x.shape
    tile = lambda n: pl.BlockSpec((tm, n), lambda i: (i, 0))
    full = lambda r, c: pl.BlockSpec((r, c), lambda i: (0, 0))
    return pl.pallas_call(
        _mix_kernel,
        out_shape=jax.ShapeDtypeStruct((t, d), F32),
        grid=(t // tm,),
        in_specs=[tile(512), tile(512), tile(d), tile(d), tile(d),
                  full(512, d), full(512, d), full(d, d), full(1, d), full(1, d)],
        out_specs=tile(d),
        compiler_params=_cparams(("parallel",)),
        name="attn_mix",
    )(a, b, ga, gb, x, wa, wb, wo, g.reshape(1, d), bb.reshape(1, d))


N_CAND = 80
BIG_IDX = 1e9


def _cand_tables():
    flat = np.full((N_CAND,), BIG_IDX, np.float32)
    valid = np.zeros((N_CAND,), np.float32)
    for i in range(16):
        flat[i] = i * 16
        valid[i] = 1
    for j in range(1, 8):
        for i in range(8):
            if (i + 1) * (j + 1) <= 16:
                r = 16 + (j - 1) * 8 + i
                flat[r] = i * 16 + j
                valid[r] = 1
    for j in range(8, 16):
        flat[72 + j - 8] = j
        valid[72 + j - 8] = 1
    return flat, valid


def _lex_argmax(v, idx):
    mx = jnp.max(v, axis=0, keepdims=True)
    mi = jnp.min(jnp.where(v == mx, idx, BIG_IDX), axis=0, keepdims=True)
    return mx, mi


def _select_kernel(x_ref, wq_ref, sk_ref, cflat_ref, cvalid_ref,
                   p1_ref, n_ref, p2_ref, r2_ref,
                   q_scr, sc_scr, work_scr, rank_scr, sort_scr, cand_scr, sel_scr):
    tt = x_ref.shape[0]
    xb = x_ref[...].astype(BF16)
    q_scr[...] = lax.dot_general(wq_ref[...], xb, NT_DIMS, preferred_element_type=F32).astype(BF16)
    key_iota = lax.broadcasted_iota(jnp.int32, (N_KEYS, tt), 0).astype(F32)
    neg_inf = jnp.float32(-jnp.inf)

    def head_body(h, carry):
        for p in range(2):
            hp = 2 * h + p
            off = pl.multiple_of(hp * 128, 128)
            sc = jnp.dot(sk_ref[hp], q_scr[pl.ds(off, 128), :], preferred_element_type=F32)
            sc_scr[p] = sc
            work_scr[...] = sc
            rank_scr[p] = jnp.full((N_KEYS, tt), float(PEER_TOPK), F32)

            def round_body(r, c):
                w = work_scr[...]
                mx, mi = _lex_argmax(w, key_iota)
                hit = key_iota == mi
                work_scr[...] = jnp.where(hit, neg_inf, w)
                rank_scr[p] = jnp.where(hit, r.astype(F32), rank_scr[p])
                sort_scr[p, pl.ds(r, 1), :] = mx
                return c

            lax.fori_loop(0, PEER_TOPK, round_body, 0)

        s1 = sort_scr[0]
        s2 = sort_scr[1]
        top = s1[0:1] + s2[0:1]
        groups = [s1 + s2[0:1]]
        for j in range(1, 8):
            groups.append(s1[0:8] + s2[j:j + 1])
        groups.append(s1[0:1] + s2[8:16])
        cflat = cflat_ref[...]
        cand_scr[...] = jnp.where(cvalid_ref[...] > 0.5, jnp.concatenate(groups, axis=0), neg_inf)
        sel_scr[...] = jnp.zeros((N_CAND, tt), F32)

        def cand_round(r, z):
            w = cand_scr[...]
            mx, mi = _lex_argmax(w, cflat)
            hit = cflat == mi
            cand_scr[...] = jnp.where(hit, neg_inf, w)
            sel_scr[...] = jnp.where(hit, 1.0, sel_scr[...])
            return z + jnp.exp(mx - top)

        z = lax.fori_loop(0, PEER_TOPK, cand_round, jnp.zeros((1, tt), F32))

        sel = sel_scr[...]
        cnt_lo = sel[0:8]
        for g in range(1, 8):
            cnt_lo = cnt_lo + sel[8 + 8 * g:16 + 8 * g]
        extra = jnp.sum(sel[72:80], axis=0, keepdims=True)
        row0 = lax.broadcasted_iota(jnp.int32, (8, tt), 0) == 0
        cnt_lo = cnt_lo + jnp.where(row0, extra, 0.0)
        cnt = jnp.concatenate([cnt_lo, sel[8:16]], axis=0)

        r1 = rank_scr[0]
        r2 = rank_scr[1]
        nsel = jnp.zeros((N_KEYS, tt), F32)
        for i in range(PEER_TOPK):
            nsel = jnp.where(r1 == float(i), cnt[i:i + 1], nsel)
        in1 = r1 < float(PEER_TOPK)
        in2 = r2 < float(PEER_TOPK)
        p1 = jnp.where(in1, jnp.exp(sc_scr[0] - s1[0:1]), 0.0) * (0.5 / z)
        p2 = jnp.where(in2, jnp.exp(sc_scr[1] - s2[0:1]), 0.0)
        p1_ref[h] = p1
        n_ref[h] = nsel
        p2_ref[h] = p2
        r2_ref[h] = r2
        return carry

    lax.fori_loop(0, PEER_HEADS, head_body, 0)


def _peer_select(x, wq_t, sk, tt=256):
    t, d = x.shape
    cflat, cvalid = _cand_tables()
    cflat = jnp.asarray(np.broadcast_to(cflat[:, None], (N_CAND, tt)))
    cvalid = jnp.asarray(np.broadcast_to(cvalid[:, None], (N_CAND, tt)))
    out = jax.ShapeDtypeStruct((PEER_HEADS, N_KEYS, t), F32)
    ospec = pl.BlockSpec((PEER_HEADS, N_KEYS, tt), lambda i: (0, 0, i))
    return pl.pallas_call(
        _select_kernel,
        out_shape=[out, out, out, out],
        grid=(t // tt,),
        in_specs=[pl.BlockSpec((tt, d), lambda i: (i, 0)),
                  pl.BlockSpec((2 * PEER_HEADS * 128, d), lambda i: (0, 0)),
                  pl.BlockSpec((2 * PEER_HEADS, N_KEYS, 128), lambda i: (0, 0, 0)),
                  pl.BlockSpec((N_CAND, tt), lambda i: (0, 0)),
                  pl.BlockSpec((N_CAND, tt), lambda i: (0, 0))],
        out_specs=[ospec, ospec, ospec, ospec],
        scratch_shapes=[pltpu.VMEM((2 * PEER_HEADS * 128, tt), BF16),
                        pltpu.VMEM((2, N_KEYS, tt), F32),
                        pltpu.VMEM((N_KEYS, tt), F32),
                        pltpu.VMEM((2, N_KEYS, tt), F32),
                        pltpu.VMEM((2, PEER_TOPK, tt), F32),
                        pltpu.VMEM((N_CAND, tt), F32),
                        pltpu.VMEM((N_CAND, tt), F32)],
        compiler_params=_cparams(("parallel",)),
        name="peer_select",
    )(x, wq_t, sk, cflat, cvalid)


SUB_E = 512


def _peer_kernel(x_ref, u_ref, vt_ref, p1_ref, n_ref, p2_ref, r2_ref, o_ref, xb_scr, w_scr, acc_scr):
    e = pl.program_id(1)
    ne = pl.num_programs(1)
    eb = u_ref.shape[0]

    @pl.when(e == 0)
    def _():
        xb_scr[...] = x_ref[...].astype(BF16)
        acc_scr[...] = jnp.zeros(acc_scr.shape, F32)

    xb = xb_scr[...]
    inv_sqrt2 = 1.0 / math.sqrt(2.0)
    for sb in range(eb // SUB_E):
        hsub = lax.dot_general(u_ref[sb * SUB_E:(sb + 1) * SUB_E, :], xb, NT_DIMS,
                               preferred_element_type=F32)
        for ai in range(SUB_E // N_KEYS):
            a = sb * (SUB_E // N_KEYS) + ai
            hh = hsub[ai * N_KEYS:(ai + 1) * N_KEYS]
            gate = None
            for h in range(PEER_HEADS):
                nrow = n_ref[h, a:a + 1, :]
                prow = p1_ref[h, a:a + 1, :]
                term = jnp.where(r2_ref[h] < nrow, p2_ref[h], 0.0) * prow
                gate = term if gate is None else gate + term
            wgt = hh * (1.0 + lax.erf(hh * inv_sqrt2)) * gate
            w_scr[a * N_KEYS:(a + 1) * N_KEYS, :] = wgt.astype(BF16)
    acc_scr[...] += jnp.dot(vt_ref[...], w_scr[...], preferred_element_type=F32)

    @pl.when(e == ne - 1)
    def _():
        o_ref[...] = acc_scr[...]


def _peer_dense(x, u, vt, p1, nsel, p2, r2, tt=256, eb=2048):
    t, d = x.shape
    ra = eb // N_KEYS
    rows = pl.BlockSpec((PEER_HEADS, ra, tt), lambda i, e: (0, e, i))
    full = pl.BlockSpec((PEER_HEADS, N_KEYS, tt), lambda i, e: (0, 0, i))
    return pl.pallas_call(
        _peer_kernel,
        out_shape=jax.ShapeDtypeStruct((d, t), F32),
        grid=(t // tt, N_EXPERTS // eb),
        in_specs=[pl.BlockSpec((tt, d), lambda i, e: (i, 0)),
                  pl.BlockSpec((eb, d), lambda i, e: (e, 0)),
                  pl.BlockSpec((d, eb), lambda i, e: (0, e)),
                  rows, rows, full, full],
        out_specs=pl.BlockSpec((d, tt), lambda i, e: (0, i)),
        scratch_shapes=[pltpu.VMEM((tt, d), BF16), pltpu.VMEM((eb, tt), BF16), pltpu.VMEM((d, tt), F32)],
        compiler_params=_cparams(("parallel", "arbitrary")),
        name="peer_dense",
    )(x, u, vt, p1, nsel, p2, r2)


def _final_kernel(x_ref, cht_ref, p_ref, wg_ref, wp_ref, g_ref, b_ref, o_ref):
    x = x_ref[...]
    gate = jax.nn.sigmoid(jnp.dot(x.astype(BF16), wg_ref[...], preferred_element_type=F32))
    ple = gate * jnp.dot(p_ref[...].astype(BF16), wp_ref[...], preferred_element_type=F32)
    ch = jnp.transpose(cht_ref[...], (1, 0))
    o_ref[...] = _layer_norm(ALPHA * x + ch + ple, g_ref[...], b_ref[...])


def _final(x, cht, p, wg, wp, g, b, tm=256):
    t, d = x.shape
    full = lambda r, c: pl.BlockSpec((r, c), lambda i: (0, 0))
    return pl.pallas_call(
        _final_kernel,
        out_shape=jax.ShapeDtypeStruct((t, d), F32),
        grid=(t // tm,),
        in_specs=[pl.BlockSpec((tm, d), lambda i: (i, 0)),
                  pl.BlockSpec((d, tm), lambda i: (0, i)),
                  pl.BlockSpec((tm, PLE_DIM), lambda i: (i, 0)),
                  full(d, d), full(PLE_DIM, d), full(1, d), full(1, d)],
        out_specs=pl.BlockSpec((tm, d), lambda i: (i, 0)),
        compiler_params=_cparams(("parallel",)),
        name="ple_ln2",
    )(x, cht, p, wg, wp, g.reshape(1, d), b.reshape(1, d))


def _diff_tile(s):
    return min(512, s)


def _trunk(x, p, prm, layers):
    b, s, d = x.shape
    t = b * s
    x = _embed_ln(x.reshape(t, d), prm["emb_g"], prm["emb_b"])
    for i, lw in enumerate(layers):
        qa, ka, va, qb, kb, vb, ga, gb = _in_proj(x, lw["w_in"])
        r3 = lambda m: m.reshape(b, s, m.shape[-1])
        a = _window_attn(r3(qa), r3(ka), r3(va), lw["bias_a"], lw["sink"])
        td = _diff_tile(s)
        bd = _diff_attn(r3(qb), r3(kb), r3(vb), lw["bias_b"][td], lw["lamv"], lw["norm_g"], lw["lam_init"], td)
        x = _attn_mix(a.reshape(t, 512), bd.reshape(t, 512), ga, gb, x,
                      lw["w_a"], lw["w_b"], lw["w_o"], lw["ln1_g"], lw["ln1_b"])
        p1, nsel, p2, r2 = _peer_select(x, lw["wq_t"], lw["subkeys"])
        cht = _peer_dense(x, lw["u"], lw["v_t"], p1, nsel, p2, r2)
        x = _final(x, cht, p[i].reshape(t, PLE_DIM), lw["ple_gate_w"], lw["ple_w"], lw["ln2_g"], lw["ln2_b"])
    return x.reshape(b, s, d)


def kernel(x_prompt, x_sample, p_prompt, p_sample, emb_ln_g, emb_ln_b, rel_bias, w_in, w_a, w_b, w_o, sink,
           lam_q1, lam_k1, lam_q2, lam_k2, diff_norm_g, ln1_g, ln1_b, peer_wq, peer_subkeys, peer_u, peer_v,
           ln2_g, ln2_b, ple_w, ple_gate_w):
    table_a = rel_bias[:, :HA]
    table_b = rel_bias[:, HA:]
    bias_a = _window_bias(table_a)
    diff_tiles = sorted({_diff_tile(x_prompt.shape[1]), _diff_tile(x_sample.shape[1])})
    bias_b = {t: _diff_bias(table_b, t) for t in diff_tiles}
    layers = []
    for i in range(DEPTH):
        layers.append(dict(
            w_in=_prep_w_in(w_in[i]),
            bias_a=bias_a,
            bias_b=bias_b,
            sink=jnp.broadcast_to(sink[i].astype(F32)[:, None], (HA, 128)),
            lamv=jnp.stack([lam_q1[i], lam_k1[i], lam_q2[i], lam_k2[i]], axis=0).astype(F32),
            norm_g=diff_norm_g[i].astype(F32).reshape(1, DVB),
            lam_init=0.8 - 0.6 * math.exp(-0.3 * i),
            w_a=w_a[i].astype(BF16), w_b=w_b[i].astype(BF16), w_o=w_o[i].astype(BF16),
            ln1_g=ln1_g[i], ln1_b=ln1_b[i],
            wq_t=jnp.transpose(peer_wq[i]).astype(BF16),
            subkeys=peer_subkeys[i].reshape(2 * PEER_HEADS, N_KEYS, 128).astype(BF16),
            u=peer_u[i].astype(BF16),
            v_t=jnp.transpose(peer_v[i]).astype(BF16),
            ple_gate_w=ple_gate_w[i].astype(BF16), ple_w=ple_w[i].astype(BF16),
            ln2_g=ln2_g[i], ln2_b=ln2_b[i],
        ))
    prm = dict(emb_g=emb_ln_g, emb_b=emb_ln_b)
    y_prompt = _trunk(x_prompt, p_prompt, prm, layers)
    y_sample = _trunk(x_sample, p_sample, prm, layers)
    return (y_prompt, y_sample)
```

```python
import functools
import math

import numpy as np
import jax
import jax.numpy as jnp
from jax import lax
from jax.experimental import pallas as pl
from jax.experimental.pallas import tpu as pltpu

F32 = jnp.float32
BF16 = jnp.bfloat16

D_MODEL = 1024
DEPTH = 2
HA, HKV, DH = 8, 2, 64
WINDOW = 128
BLK = 128
HB = 4
DVB = 2 * DH
N_BUCKETS = 32
MAX_DIST = 128
N_KEYS = 128
N_EXPERTS = N_KEYS * N_KEYS
PEER_HEADS = 8
PEER_TOPK = 16
PLE_DIM = 256
ALPHA = (2.0 * DEPTH) ** 0.25
LN_EPS = 1e-5
NEG = -1e30

VMEM_LIMIT = 56 * 1024 * 1024
BF16_TILE = 16

NT_DIMS = (((1,), (1,)), ((), ()))


def _cparams(sem):
    return pltpu.CompilerParams(dimension_semantics=sem, vmem_limit_bytes=VMEM_LIMIT)


def _layer_norm(x, g, b):
    mu = jnp.mean(x, axis=-1, keepdims=True)
    xc = x - mu
    var = jnp.mean(xc * xc, axis=-1, keepdims=True)
    return xc * lax.rsqrt(var + LN_EPS) * g + b


def _ln_kernel(x_ref, g_ref, b_ref, o_ref):
    o_ref[...] = _layer_norm(x_ref[...], g_ref[...], b_ref[...])


def _embed_ln(x, g, b, tm=512):
    t, d = x.shape
    row = pl.BlockSpec((1, d), lambda i: (0, 0))
    return pl.pallas_call(
        _ln_kernel,
        out_shape=jax.ShapeDtypeStruct((t, d), F32),
        grid=(t // tm,),
        in_specs=[pl.BlockSpec((tm, d), lambda i: (i, 0)), row, row],
        out_specs=pl.BlockSpec((tm, d), lambda i: (i, 0)),
        compiler_params=_cparams(("parallel",)),
        name="embed_ln",
    )(x, g.reshape(1, d), b.reshape(1, d))


_QA0, _KA0, _VA0, _QB0, _KB0, _VB0, _GA0, _GB0, _INW = 0, 512, 768, 1024, 1536, 2048, 2560, 3584, 4608


def _inproj_kernel(x_ref, w_ref, qa_ref, ka_ref, va_ref, qb_ref, kb_ref, vb_ref, ga_ref, gb_ref):
    x = x_ref[...].astype(BF16)

    def proj(lo, hi):
        return jnp.dot(x, w_ref[:, lo:hi], preferred_element_type=F32)

    qa_ref[...] = proj(_QA0, _KA0).astype(BF16)
    ka_ref[...] = proj(_KA0, _VA0).astype(BF16)
    va_ref[...] = proj(_VA0, _QB0).astype(BF16)
    qb_ref[...] = proj(_QB0, _KB0).astype(BF16)
    kb_ref[...] = proj(_KB0, _VB0).astype(BF16)
    vb_ref[...] = proj(_VB0, _GA0).astype(BF16)
    ga_ref[...] = jax.nn.sigmoid(proj(_GA0, _GB0)).astype(BF16)
    gb_ref[...] = jax.nn.sigmoid(proj(_GB0, _INW)).astype(BF16)


def _in_proj(x, w, tm=512):
    t, d = x.shape
    widths = (512, 256, 256, 512, 512, 512, 1024, 1024)
    return pl.pallas_call(
        _inproj_kernel,
        out_shape=[jax.ShapeDtypeStruct((t, n), BF16) for n in widths],
        grid=(t // tm,),
        in_specs=[pl.BlockSpec((tm, d), lambda i: (i, 0)), pl.BlockSpec((d, _INW), lambda i: (0, 0))],
        out_specs=[pl.BlockSpec((tm, n), lambda i: (i, 0)) for n in widths],
        compiler_params=_cparams(("parallel",)),
        name="in_proj",
    )(x, w)


def _prep_w_in(w):
    sc = DH ** -0.5
    qa = w[:, 0:512] * sc
    ka = w[:, 512:640]
    va = w[:, 640:768]
    qb = w[:, 768:1280] * sc
    rest = w[:, 1280:]
    dup = lambda m: jnp.concatenate([m[:, 0:64], m[:, 0:64], m[:, 64:128], m[:, 64:128]], axis=1)
    return jnp.concatenate([qa, dup(ka), dup(va), qb, rest], axis=1).astype(BF16)


def _t5_bucket(rel):
    half = N_BUCKETS // 2
    max_exact = half // 2
    ret = jnp.where(rel > 0, half, 0)
    n = jnp.abs(rel)
    nf = jnp.maximum(n, 1).astype(F32)
    large = max_exact + (jnp.log(nf / max_exact) / math.log(MAX_DIST / max_exact)
                         * (half - max_exact)).astype(jnp.int32)
    large = jnp.minimum(large, half - 1)
    return ret + jnp.where(n < max_exact, n, large)


def _bias_lookup(table, bucket):
    tab = table.astype(F32)
    out = jnp.zeros((tab.shape[1],) + bucket.shape, F32)
    for bk in range(N_BUCKETS):
        out = jnp.where((bucket == bk)[None], tab[bk].reshape((-1,) + (1,) * bucket.ndim), out)
    return out


def _window_bias(table_a):
    i = jnp.arange(BLK, dtype=jnp.int32)[:, None]
    j = jnp.arange(3 * BLK, dtype=jnp.int32)[None, :]
    rel = j - BLK - i
    bias = _bias_lookup(table_a, _t5_bucket(rel))
    return jnp.where((jnp.abs(rel) <= WINDOW)[None], bias, NEG)


def _diff_bias(table_b, t):
    i = jnp.arange(t, dtype=jnp.int32)[:, None]
    j = jnp.arange(t, dtype=jnp.int32)[None, :]
    tiles = []
    for d in (-2, -1, 0, 1, 2):
        rel = d * t + j - i
        tiles.append(_bias_lookup(table_b, _t5_bucket(rel)))
    return jnp.stack(tiles, axis=0)


def _window_kernel(q_ref, kp_ref, kc_ref, kn_ref, vp_ref, vc_ref, vn_ref, bias_ref, sink_ref, o_ref):
    n = pl.program_id(1)
    nb = pl.num_programs(1)
    q = q_ref[0]
    k3 = jnp.concatenate([kp_ref[0], kc_ref[0], kn_ref[0]], axis=0)
    v3 = jnp.concatenate([vp_ref[0], vc_ref[0], vn_ref[0]], axis=0)
    lane = lax.broadcasted_iota(jnp.int32, (1, 128), 1)
    lo = lane < DH
    col = lax.broadcasted_iota(jnp.int32, (1, 3 * BLK), 1)
    off_edge = ((col < BLK) & (n == 0)) | ((col >= 2 * BLK) & (n == nb - 1))
    edge = jnp.where(off_edge, NEG, 0.0).astype(F32)
    zero = jnp.zeros((), BF16)
    for pair in range(HA // 2):
        kvh = (2 * pair) // (HA // HKV)
        q2 = q[:, 128 * pair:128 * (pair + 1)]
        kk = k3[:, 128 * kvh:128 * (kvh + 1)]
        vv = v3[:, 128 * kvh:128 * (kvh + 1)]
        outs = []
        for half in range(2):
            hd = 2 * pair + half
            qm = jnp.where(lo if half == 0 else jnp.logical_not(lo), q2, zero)
            s = lax.dot_general(qm, kk, NT_DIMS, preferred_element_type=F32)
            s = s + bias_ref[hd] + edge
            sk = sink_ref[hd:hd + 1, 0:1]
            m = jnp.maximum(jnp.max(s, axis=-1, keepdims=True), sk)
            p = jnp.exp(s - m)
            denom = jnp.sum(p, axis=-1, keepdims=True) + jnp.exp(sk - m)
            o = jnp.dot(p.astype(BF16), vv, preferred_element_type=F32)
            outs.append(o / denom)
        o_ref[0, :, 128 * pair:128 * (pair + 1)] = jnp.where(lo, outs[0], outs[1]).astype(BF16)


def _window_attn(qa, ka, va, bias_a, sink_b):
    b, s, _ = qa.shape
    nb = s // BLK
    kv = lambda f: pl.BlockSpec((1, BLK, 256), f)
    prev = lambda bi, n: (bi, jnp.maximum(n - 1, 0), 0)
    cur = lambda bi, n: (bi, n, 0)
    nxt = lambda bi, n: (bi, jnp.minimum(n + 1, nb - 1), 0)
    return pl.pallas_call(
        _window_kernel,
        out_shape=jax.ShapeDtypeStruct((b, s, 512), BF16),
        grid=(b, nb),
        in_specs=[pl.BlockSpec((1, BLK, 512), cur), kv(prev), kv(cur), kv(nxt), kv(prev), kv(cur), kv(nxt),
                  pl.BlockSpec((HA, BLK, 3 * BLK), lambda bi, n: (0, 0, 0)),
                  pl.BlockSpec((HA, 128), lambda bi, n: (0, 0))],
        out_specs=pl.BlockSpec((1, BLK, 512), cur),
        compiler_params=_cparams(("parallel", "arbitrary")),
        name="window_attn",
    )(qa, ka, ka, ka, va, va, va, bias_a, sink_b)


def _diff_kernel(q_ref, k_ref, v_ref, bias_ref, lamv_ref, g_ref, o_ref, m_scr, acc_scr, *, lam_init):
    j = pl.program_id(2)
    nk = pl.num_programs(2)
    tq = q_ref.shape[1]
    tk = k_ref.shape[1]

    @pl.when(j == 0)
    def _():
        m_scr[...] = jnp.full(m_scr.shape, -jnp.inf, F32)
        acc_scr[...] = jnp.zeros(acc_scr.shape, F32)

    lane = lax.broadcasted_iota(jnp.int32, (1, 128), 1)
    lo = lane < DH
    zero = jnp.zeros((), BF16)
    ones = jnp.ones((tk, 128), BF16)
    for h in range(HB):
        qh = q_ref[0, :, 128 * h:128 * (h + 1)]
        kh = k_ref[0, :, 128 * h:128 * (h + 1)]
        vaug = jnp.concatenate([v_ref[0, :, 128 * h:128 * (h + 1)], ones], axis=1)
        bias = bias_ref[0, h]
        for c in range(2):
            idx = 2 * h + c
            qm = jnp.where(lo if c == 0 else jnp.logical_not(lo), qh, zero)
            s = lax.dot_general(qm, kh, NT_DIMS, preferred_element_type=F32) + bias
            m_old = m_scr[idx]
            m_new = jnp.maximum(m_old, jnp.max(s, axis=-1, keepdims=True))
            p = jnp.exp(s - m_new)
            acc_scr[idx] = acc_scr[idx] * jnp.exp(m_old - m_new) + jnp.dot(
                p.astype(BF16), vaug, preferred_element_type=F32)
            m_scr[idx] = m_new

    @pl.when(j == nk - 1)
    def _():
        lv = lamv_ref[...]
        dot1 = jnp.sum(lv[0:1] * lv[1:2], axis=-1, keepdims=True)
        dot2 = jnp.sum(lv[2:3] * lv[3:4], axis=-1, keepdims=True)
        lam = jnp.exp(dot1) - jnp.exp(dot2) + lam_init
        for h in range(HB):
            a0 = acc_scr[2 * h]
            a1 = acc_scr[2 * h + 1]
            o = a0[:, :128] / a0[:, 128:] - lam * (a1[:, :128] / a1[:, 128:])
            o = o * lax.rsqrt(jnp.mean(o * o, axis=-1, keepdims=True) + LN_EPS)
            o = o * g_ref[...] * (1.0 - lam_init)
            o_ref[0, :, 128 * h:128 * (h + 1)] = o.astype(BF16)


def _diff_attn(qb, kb, vb, bias5, lamv, norm_g, lam_init, t):
    b, s, _ = qb.shape
    nt = s // t
    return pl.pallas_call(
        functools.partial(_diff_kernel, lam_init=lam_init),
        out_shape=jax.ShapeDtypeStruct((b, s, 512), BF16),
        grid=(b, nt, nt),
        in_specs=[pl.BlockSpec((1, t, 512), lambda bi, i, j: (bi, i, 0)),
                  pl.BlockSpec((1, t, 512), lambda bi, i, j: (bi, j, 0)),
                  pl.BlockSpec((1, t, 512), lambda bi, i, j: (bi, j, 0)),
                  pl.BlockSpec((1, HB, t, t), lambda bi, i, j: (jnp.clip(j - i, -2, 2) + 2, 0, 0, 0)),
                  pl.BlockSpec((4, DH), lambda bi, i, j: (0, 0)),
                  pl.BlockSpec((1, DVB), lambda bi, i, j: (0, 0))],
        out_specs=pl.BlockSpec((1, t, 512), lambda bi, i, j: (bi, i, 0)),
        scratch_shapes=[pltpu.VMEM((2 * HB, t, 1), F32), pltpu.VMEM((2 * HB, t, 256), F32)],
        compiler_params=_cparams(("parallel", "parallel", "arbitrary")),
        name="diff_attn",
    )(qb, kb, vb, bias5, lamv, norm_g)


def _mix_kernel(a_ref, b_ref, ga_ref, gb_ref, x_ref, wa_ref, wb_ref, wo_ref, g_ref, bb_ref, o_ref):
    a = jnp.dot(a_ref[...], wa_ref[...], preferred_element_type=F32)
    b = jnp.dot(b_ref[...], wb_ref[...], preferred_element_type=F32)
    mix_in = ga_ref[...].astype(F32) * a + gb_ref[...].astype(F32) * b
    mix = jnp.dot(mix_in.astype(BF16), wo_ref[...], preferred_element_type=F32)
    o_ref[...] = _layer_norm(ALPHA * x_ref[...] + mix, g_ref[...], bb_ref[...])


def _attn_mix(a, b, ga, gb, x, wa, wb, wo, g, bb, tm=512):
    t, d = x.shape
    tile = lambda n: pl.BlockSpec((tm, n), lambda i: (i, 0))
    full = lambda r, c: pl.BlockSpec((r, c), lambda i: (0, 0))
    return pl.pallas_call(
        _mix_kernel,
        out_shape=jax.ShapeDtypeStruct((t, d), F32),
        grid=(t // tm,),
        in_specs=[tile(512), tile(512), tile(d), tile(d), tile(d),
                  full(512, d), full(512, d), full(d, d), full(1, d), full(1, d)],
        out_specs=tile(d),
        compiler_params=_cparams(("parallel",)),
        name="attn_mix",
    )(a, b, ga, gb, x, wa, wb, wo, g.reshape(1, d), bb.reshape(1, d))


N_CAND = 80
BIG_IDX = 1e9


def _cand_tables():
    flat = np.full((N_CAND,), BIG_IDX, np.float32)
    valid = np.zeros((N_CAND,), np.float32)
    for i in range(16):
        flat[i] = i * 16
        valid[i] = 1
    for j in range(1, 8):
        for i in range(8):
            if (i + 1) * (j + 1) <= 16:
                r = 16 + (j - 1) * 8 + i
                flat[r] = i * 16 + j
                valid[r] = 1
    for j in range(8, 16):
        flat[72 + j - 8] = j
        valid[72 + j - 8] = 1
    return flat, valid


def _lex_argmax(v, idx):
    mx = jnp.max(v, axis=0, keepdims=True)
    mi = jnp.min(jnp.where(v == mx, idx, BIG_IDX), axis=0, keepdims=True)
    return mx, mi


def _select_kernel(x_ref, wq_ref, sk_ref, cflat_ref, cvalid_ref,
                   p1_ref, n_ref, p2_ref, r2_ref,
                   q_scr, sc_scr, work_scr, rank_scr, sort_scr, cand_scr, sel_scr):
    tt = x_ref.shape[0]
    xb = x_ref[...].astype(BF16)
    q_scr[...] = lax.dot_general(wq_ref[...], xb, NT_DIMS, preferred_element_type=F32).astype(BF16)
    key_iota = lax.broadcasted_iota(jnp.int32, (N_KEYS, tt), 0).astype(F32)
    neg_inf = jnp.float32(-jnp.inf)

    def head_body(h, carry):
        for p in range(2):
            hp = 2 * h + p
            off = pl.multiple_of(hp * 128, 128)
            sc = jnp.dot(sk_ref[hp], q_scr[pl.ds(off, 128), :], preferred_element_type=F32)
            sc_scr[p] = sc
            work_scr[...] = sc
            rank_scr[p] = jnp.full((N_KEYS, tt), float(PEER_TOPK), F32)

            def round_body(r, c):
                w = work_scr[...]
                mx, mi = _lex_argmax(w, key_iota)
                hit = key_iota == mi
                work_scr[...] = jnp.where(hit, neg_inf, w)
                rank_scr[p] = jnp.where(hit, jnp.asarray(r, F32), rank_scr[p])
                sort_scr[p, pl.ds(r, 1), :] = mx
                return c

            lax.fori_loop(0, PEER_TOPK, round_body, 0)

        s1 = sort_scr[0]
        s2 = sort_scr[1]
        top = s1[0:1] + s2[0:1]
        groups = [s1 + s2[0:1]]
        for j in range(1, 8):
            groups.append(s1[0:8] + s2[j:j + 1])
        groups.append(s1[0:1] + s2[8:16])
        cflat = cflat_ref[...]
        cand_scr[...] = jnp.where(cvalid_ref[...] > 0.5, jnp.concatenate(groups, axis=0), neg_inf)
        sel_scr[...] = jnp.zeros((N_CAND, tt), F32)

        def cand_round(r, z):
            w = cand_scr[...]
            mx, mi = _lex_argmax(w, cflat)
            hit = cflat == mi
            cand_scr[...] = jnp.where(hit, neg_inf, w)
            sel_scr[...] = jnp.where(hit, 1.0, sel_scr[...])
            return z + jnp.exp(mx - top)

        z = lax.fori_loop(0, PEER_TOPK, cand_round, jnp.zeros((1, tt), F32))

        sel = sel_scr[...]
        cnt_lo = sel[0:8]
        for g in range(1, 8):
            cnt_lo = cnt_lo + sel[8 + 8 * g:16 + 8 * g]
        extra = jnp.sum(sel[72:80], axis=0, keepdims=True)
        row0 = lax.broadcasted_iota(jnp.int32, (8, tt), 0) == 0
        cnt_lo = cnt_lo + jnp.where(row0, extra, 0.0)
        cnt = jnp.concatenate([cnt_lo, sel[8:16]], axis=0)

        r1 = rank_scr[0]
        r2 = rank_scr[1]
        nsel = jnp.zeros((N_KEYS, tt), F32)
        for i in range(PEER_TOPK):
            nsel = jnp.where(r1 == float(i), cnt[i:i + 1], nsel)
        in1 = r1 < float(PEER_TOPK)
        in2 = r2 < float(PEER_TOPK)
        p1 = jnp.where(in1, jnp.exp(sc_scr[0] - s1[0:1]), 0.0) * (0.5 / z)
        p2 = jnp.where(in2, jnp.exp(sc_scr[1] - s2[0:1]), 0.0)
        p1_ref[h] = p1
        n_ref[h] = nsel
        p2_ref[h] = p2.astype(BF16).reshape(N_KEYS // BF16_TILE, BF16_TILE, tt)
        r2_ref[h] = r2.astype(BF16).reshape(N_KEYS // BF16_TILE, BF16_TILE, tt)
        return carry

    lax.fori_loop(0, PEER_HEADS, head_body, 0)


def _peer_select(x, wq_t, sk, tt=256):
    t, d = x.shape
    cflat, cvalid = _cand_tables()
    cflat = jnp.asarray(np.broadcast_to(cflat[:, None], (N_CAND, tt)))
    cvalid = jnp.asarray(np.broadcast_to(cvalid[:, None], (N_CAND, tt)))
    out = jax.ShapeDtypeStruct((PEER_HEADS, N_KEYS, t), F32)
    out16 = jax.ShapeDtypeStruct((PEER_HEADS, N_KEYS // BF16_TILE, BF16_TILE, t), BF16)
    ospec = pl.BlockSpec((PEER_HEADS, N_KEYS, tt), lambda i: (0, 0, i))
    ospec16 = pl.BlockSpec((PEER_HEADS, N_KEYS // BF16_TILE, BF16_TILE, tt), lambda i: (0, 0, 0, i))
    return pl.pallas_call(
        _select_kernel,
        out_shape=[out, out, out16, out16],
        grid=(t // tt,),
        in_specs=[pl.BlockSpec((tt, d), lambda i: (i, 0)),
                  pl.BlockSpec((2 * PEER_HEADS * 128, d), lambda i: (0, 0)),
                  pl.BlockSpec((2 * PEER_HEADS, N_KEYS, 128), lambda i: (0, 0, 0)),
                  pl.BlockSpec((N_CAND, tt), lambda i: (0, 0)),
                  pl.BlockSpec((N_CAND, tt), lambda i: (0, 0))],
        out_specs=[ospec, ospec, ospec16, ospec16],
        scratch_shapes=[pltpu.VMEM((2 * PEER_HEADS * 128, tt), BF16),
                        pltpu.VMEM((2, N_KEYS, tt), F32),
                        pltpu.VMEM((N_KEYS, tt), F32),
                        pltpu.VMEM((2, N_KEYS, tt), F32),
                        pltpu.VMEM((2, PEER_TOPK, tt), F32),
                        pltpu.VMEM((N_CAND, tt), F32),
                        pltpu.VMEM((N_CAND, tt), F32)],
        compiler_params=_cparams(("parallel",)),
        name="peer_select",
    )(x, wq_t, sk, cflat, cvalid)


SUB_E = 512


def _peer_kernel(x_ref, u_ref, vt_ref, p1_ref, n_ref, p2_ref, r2_ref, o_ref, xb_scr, w_scr, acc_scr):
    e = pl.program_id(1)
    ne = pl.num_programs(1)
    eb = u_ref.shape[0]
    tt = x_ref.shape[0]

    @pl.when(e == 0)
    def _():
        xb_scr[...] = x_ref[...].astype(BF16)
        acc_scr[...] = jnp.zeros(acc_scr.shape, F32)

    xb = xb_scr[...]
    inv_sqrt2 = 1.0 / math.sqrt(2.0)
    for sb in range(eb // SUB_E):
        hsub = lax.dot_general(u_ref[sb * SUB_E:(sb + 1) * SUB_E, :], xb, NT_DIMS,
                               preferred_element_type=F32)
        for ai in range(SUB_E // N_KEYS):
            a = sb * (SUB_E // N_KEYS) + ai
            hh = hsub[ai * N_KEYS:(ai + 1) * N_KEYS]
            gate = None
            for h in range(PEER_HEADS):
                nrow = jnp.broadcast_to(n_ref[h, a:a + 1, :], (BF16_TILE, tt)).astype(BF16)[None]
                prow = jnp.broadcast_to(p1_ref[h, a:a + 1, :], (BF16_TILE, tt)).astype(BF16)[None]
                term = jnp.where(r2_ref[h] < nrow, p2_ref[h], jnp.zeros((), BF16)) * prow
                gate = term if gate is None else gate + term
            act = hh * (1.0 + lax.erf(hh * inv_sqrt2))
            wgt = act.astype(BF16).reshape(N_KEYS // BF16_TILE, BF16_TILE, tt) * gate
            w_scr[a * N_KEYS:(a + 1) * N_KEYS, :] = wgt.reshape(N_KEYS, tt)
    acc_scr[...] += jnp.dot(vt_ref[...], w_scr[...], preferred_element_type=F32)

    @pl.when(e == ne - 1)
    def _():
        o_ref[...] = acc_scr[...]


def _peer_dense(x, u, vt, p1, nsel, p2, r2, tt=512, eb=2048):
    t, d = x.shape
    ra = eb // N_KEYS
    rows = pl.BlockSpec((PEER_HEADS, ra, tt), lambda i, e: (0, e, i))
    full = pl.BlockSpec((PEER_HEADS, N_KEYS // BF16_TILE, BF16_TILE, tt), lambda i, e: (0, 0, 0, i))
    return pl.pallas_call(
        _peer_kernel,
        out_shape=jax.ShapeDtypeStruct((d, t), F32),
        grid=(t // tt, N_EXPERTS // eb),
        in_specs=[pl.BlockSpec((tt, d), lambda i, e: (i, 0)),
                  pl.BlockSpec((eb, d), lambda i, e: (e, 0)),
                  pl.BlockSpec((d, eb), lambda i, e: (0, e)),
                  rows, rows, full, full],
        out_specs=pl.BlockSpec((d, tt), lambda i, e: (0, i)),
        scratch_shapes=[pltpu.VMEM((tt, d), BF16), pltpu.VMEM((eb, tt), BF16), pltpu.VMEM((d, tt), F32)],
        compiler_params=_cparams(("parallel", "arbitrary")),
        name="peer_dense",
    )(x, u, vt, p1, nsel, p2, r2)


def _final_kernel(x_ref, cht_ref, p_ref, wg_ref, wp_ref, g_ref, b_ref, o_ref):
    x = x_ref[...]
    gate = jax.nn.sigmoid(jnp.dot(x.astype(BF16), wg_ref[...], preferred_element_type=F32))
    ple = gate * jnp.dot(p_ref[...].astype(BF16), wp_ref[...], preferred_element_type=F32)
    ch = jnp.transpose(cht_ref[...], (1, 0))
    o_ref[...] = _layer_norm(ALPHA * x + ch + ple, g_ref[...], b_ref[...])


def _final(x, cht, p, wg, wp, g, b, tm=256):
    t, d = x.shape
    full = lambda r, c: pl.BlockSpec((r, c), lambda i: (0, 0))
    return pl.pallas_call(
        _final_kernel,
        out_shape=jax.ShapeDtypeStruct((t, d), F32),
        grid=(t // tm,),
        in_specs=[pl.BlockSpec((tm, d), lambda i: (i, 0)),
                  pl.BlockSpec((d, tm), lambda i: (0, i)),
                  pl.BlockSpec((tm, PLE_DIM), lambda i: (i, 0)),
                  full(d, d), full(PLE_DIM, d), full(1, d), full(1, d)],
        out_specs=pl.BlockSpec((tm, d), lambda i: (i, 0)),
        compiler_params=_cparams(("parallel",)),
        name="ple_ln2",
    )(x, cht, p, wg, wp, g.reshape(1, d), b.reshape(1, d))


def _diff_tile(s):
    return min(512, s)


def _trunk(x, p, prm, layers):
    b, s, d = x.shape
    t = b * s
    x = _embed_ln(x.reshape(t, d), prm["emb_g"], prm["emb_b"])
    for i, lw in enumerate(layers):
        qa, ka, va, qb, kb, vb, ga, gb = _in_proj(x, lw["w_in"])
        r3 = lambda m: m.reshape(b, s, m.shape[-1])
        a = _window_attn(r3(qa), r3(ka), r3(va), lw["bias_a"], lw["sink"])
        td = _diff_tile(s)
        bd = _diff_attn(r3(qb), r3(kb), r3(vb), lw["bias_b"][td], lw["lamv"], lw["norm_g"], lw["lam_init"], td)
        x = _attn_mix(a.reshape(t, 512), bd.reshape(t, 512), ga, gb, x,
                      lw["w_a"], lw["w_b"], lw["w_o"], lw["ln1_g"], lw["ln1_b"])
        p1, nsel, p2, r2 = _peer_select(x, lw["wq_t"], lw["subkeys"])
        cht = _peer_dense(x, lw["u"], lw["v_t"], p1, nsel, p2, r2)
        x = _final(x, cht, p[i].reshape(t, PLE_DIM), lw["ple_gate_w"], lw["ple_w"], lw["ln2_g"], lw["ln2_b"])
    return x.reshape(b, s, d)


def kernel(x_prompt, x_sample, p_prompt, p_sample, emb_ln_g, emb_ln_b, rel_bias, w_in, w_a, w_b, w_o, sink,
           lam_q1, lam_k1, lam_q2, lam_k2, diff_norm_g, ln1_g, ln1_b, peer_wq, peer_subkeys, peer_u, peer_v,
           ln2_g, ln2_b, ple_w, ple_gate_w):
    table_a = rel_bias[:, :HA]
    table_b = rel_bias[:, HA:]
    bias_a = _window_bias(table_a)
    diff_tiles = sorted({_diff_tile(x_prompt.shape[1]), _diff_tile(x_sample.shape[1])})
    bias_b = {t: _diff_bias(table_b, t) for t in diff_tiles}
    layers = []
    for i in range(DEPTH):
        layers.append(dict(
            w_in=_prep_w_in(w_in[i]),
            bias_a=bias_a,
            bias_b=bias_b,
            sink=jnp.broadcast_to(sink[i].astype(F32)[:, None], (HA, 128)),
            lamv=jnp.stack([lam_q1[i], lam_k1[i], lam_q2[i], lam_k2[i]], axis=0).astype(F32),
            norm_g=diff_norm_g[i].astype(F32).reshape(1, DVB),
            lam_init=0.8 - 0.6 * math.exp(-0.3 * i),
            w_a=w_a[i].astype(BF16), w_b=w_b[i].astype(BF16), w_o=w_o[i].astype(BF16),
            ln1_g=ln1_g[i], ln1_b=ln1_b[i],
            wq_t=jnp.transpose(peer_wq[i]).astype(BF16),
            subkeys=peer_subkeys[i].reshape(2 * PEER_HEADS, N_KEYS, 128).astype(BF16),
            u=peer_u[i].astype(BF16),
            v_t=jnp.transpose(peer_v[i]).astype(BF16),
            ple_gate_w=ple_gate_w[i].astype(BF16), ple_w=ple_w[i].astype(BF16),
            ln2_g=ln2_g[i], ln2_b=ln2_b[i],
        ))
    prm = dict(emb_g=emb_ln_g, emb_b=emb_ln_b)
    y_prompt = _trunk(x_prompt, p_prompt, prm, layers)
    y_sample = _trunk(x_sample, p_sample, prm, layers)
    return (y_prompt, y_sample)
```

```python
import functools
import math

import numpy as np
import jax
import jax.numpy as jnp
from jax import lax
from jax.experimental import pallas as pl
from jax.experimental.pallas import tpu as pltpu

F32 = jnp.float32
BF16 = jnp.bfloat16

D_MODEL = 1024
DEPTH = 2
HA, HKV, DH = 8, 2, 64
WINDOW = 128
BLK = 128
HB = 4
DVB = 2 * DH
N_BUCKETS = 32
MAX_DIST = 128
N_KEYS = 128
N_EXPERTS = N_KEYS * N_KEYS
PEER_HEADS = 8
PEER_TOPK = 16
PLE_DIM = 256
ALPHA = (2.0 * DEPTH) ** 0.25
LN_EPS = 1e-5
NEG = -1e30
LOG2E = math.log2(math.e)

VMEM_LIMIT = 56 * 1024 * 1024
BF16_TILE = 16

NT_DIMS = (((1,), (1,)), ((), ()))


def _cparams(sem):
    return pltpu.CompilerParams(dimension_semantics=sem, vmem_limit_bytes=VMEM_LIMIT)


def _layer_norm(x, g, b):
    mu = jnp.mean(x, axis=-1, keepdims=True)
    xc = x - mu
    var = jnp.mean(xc * xc, axis=-1, keepdims=True)
    return xc * lax.rsqrt(var + LN_EPS) * g + b


def _ln_kernel(x_ref, g_ref, b_ref, o_ref):
    o_ref[...] = _layer_norm(x_ref[...], g_ref[...], b_ref[...])


def _embed_ln(x, g, b, tm=512):
    t, d = x.shape
    row = pl.BlockSpec((1, d), lambda i: (0, 0))
    return pl.pallas_call(
        _ln_kernel,
        out_shape=jax.ShapeDtypeStruct((t, d), F32),
        grid=(t // tm,),
        in_specs=[pl.BlockSpec((tm, d), lambda i: (i, 0)), row, row],
        out_specs=pl.BlockSpec((tm, d), lambda i: (i, 0)),
        compiler_params=_cparams(("parallel",)),
        name="embed_ln",
    )(x, g.reshape(1, d), b.reshape(1, d))


_QA0, _KA0, _VA0, _QB0, _KB0, _GA0, _GB0, _INW = 0, 512, 768, 1024, 1536, 2048, 3072, 4096


def _inproj_kernel(x_ref, w_ref, wvt_ref, qa_ref, ka_ref, va_ref, qb_ref, kb_ref, vbt_ref, ga_ref, gb_ref):
    x = x_ref[...].astype(BF16)

    def proj(lo, hi):
        return jnp.dot(x, w_ref[:, lo:hi], preferred_element_type=F32)

    qa_ref[...] = proj(_QA0, _KA0).astype(BF16)
    ka_ref[...] = proj(_KA0, _VA0).astype(BF16)
    va_ref[...] = proj(_VA0, _QB0).astype(BF16)
    qb_ref[...] = proj(_QB0, _KB0).astype(BF16)
    kb_ref[...] = proj(_KB0, _GA0).astype(BF16)
    vbt_ref[...] = lax.dot_general(wvt_ref[...], x, NT_DIMS, preferred_element_type=F32).astype(BF16)
    ga_ref[...] = jax.nn.sigmoid(proj(_GA0, _GB0)).astype(BF16)
    gb_ref[...] = jax.nn.sigmoid(proj(_GB0, _INW)).astype(BF16)


def _in_proj(x, w, wvt, tm=512):
    t, d = x.shape
    rows = lambda n: (jax.ShapeDtypeStruct((t, n), BF16), pl.BlockSpec((tm, n), lambda i: (i, 0)))
    outs = [rows(512), rows(256), rows(256), rows(512), rows(512),
            (jax.ShapeDtypeStruct((HB * DVB, t), BF16), pl.BlockSpec((HB * DVB, tm), lambda i: (0, i))),
            rows(d), rows(d)]
    return pl.pallas_call(
        _inproj_kernel,
        out_shape=[o[0] for o in outs],
        grid=(t // tm,),
        in_specs=[pl.BlockSpec((tm, d), lambda i: (i, 0)), pl.BlockSpec((d, _INW), lambda i: (0, 0)),
                  pl.BlockSpec((HB * DVB, d), lambda i: (0, 0))],
        out_specs=[o[1] for o in outs],
        compiler_params=_cparams(("parallel",)),
        name="in_proj",
    )(x, w, wvt)


def _prep_w_in(w):
    sc = DH ** -0.5
    qa = w[:, 0:512] * sc
    ka = w[:, 512:640]
    va = w[:, 640:768]
    qb = w[:, 768:1280] * (sc * LOG2E)
    kb = w[:, 1280:1792]
    vb = w[:, 1792:2304]
    gates = w[:, 2304:]
    dup = lambda m: jnp.concatenate([m[:, 0:64], m[:, 0:64], m[:, 64:128], m[:, 64:128]], axis=1)
    main = jnp.concatenate([qa, dup(ka), dup(va), qb, kb, gates], axis=1).astype(BF16)
    return main, jnp.transpose(vb).astype(BF16)


def _t5_bucket(rel):
    half = N_BUCKETS // 2
    max_exact = half // 2
    ret = jnp.where(rel > 0, half, 0)
    n = jnp.abs(rel)
    nf = jnp.maximum(n, 1).astype(F32)
    large = max_exact + (jnp.log(nf / max_exact) / math.log(MAX_DIST / max_exact)
                         * (half - max_exact)).astype(jnp.int32)
    large = jnp.minimum(large, half - 1)
    return ret + jnp.where(n < max_exact, n, large)


def _bias_lookup(table, bucket):
    tab = table.astype(F32)
    out = jnp.zeros((tab.shape[1],) + bucket.shape, F32)
    for bk in range(N_BUCKETS):
        out = jnp.where((bucket == bk)[None], tab[bk].reshape((-1,) + (1,) * bucket.ndim), out)
    return out


def _window_bias(table_a):
    i = jnp.arange(BLK, dtype=jnp.int32)[:, None]
    j = jnp.arange(3 * BLK, dtype=jnp.int32)[None, :]
    rel = j - BLK - i
    bias = _bias_lookup(table_a, _t5_bucket(rel))
    return jnp.where((jnp.abs(rel) <= WINDOW)[None], bias, NEG)


def _diff_bias(table_b, t):
    kpos = jnp.arange(t, dtype=jnp.int32)[:, None]
    qpos = jnp.arange(t, dtype=jnp.int32)[None, :]
    tiles = []
    for d in (-2, -1, 0, 1, 2):
        rel = d * t + kpos - qpos
        tiles.append(_bias_lookup(table_b, _t5_bucket(rel)) * LOG2E)
    return jnp.stack(tiles, axis=0)


def _window_kernel(q_ref, kp_ref, kc_ref, kn_ref, vp_ref, vc_ref, vn_ref, bias_ref, sink_ref, o_ref):
    n = pl.program_id(1)
    nb = pl.num_programs(1)
    q = q_ref[0]
    k3 = jnp.concatenate([kp_ref[0], kc_ref[0], kn_ref[0]], axis=0)
    v3 = jnp.concatenate([vp_ref[0], vc_ref[0], vn_ref[0]], axis=0)
    lane = lax.broadcasted_iota(jnp.int32, (1, 128), 1)
    lo = lane < DH
    col = lax.broadcasted_iota(jnp.int32, (1, 3 * BLK), 1)
    off_edge = ((col < BLK) & (n == 0)) | ((col >= 2 * BLK) & (n == nb - 1))
    edge = jnp.where(off_edge, NEG, 0.0).astype(F32)
    zero = jnp.zeros((), BF16)
    for pair in range(HA // 2):
        kvh = (2 * pair) // (HA // HKV)
        q2 = q[:, 128 * pair:128 * (pair + 1)]
        kk = k3[:, 128 * kvh:128 * (kvh + 1)]
        vv = v3[:, 128 * kvh:128 * (kvh + 1)]
        outs = []
        for half in range(2):
            hd = 2 * pair + half
            qm = jnp.where(lo if half == 0 else jnp.logical_not(lo), q2, zero)
            s = lax.dot_general(qm, kk, NT_DIMS, preferred_element_type=F32)
            s = s + bias_ref[hd] + edge
            sk = sink_ref[hd:hd + 1, 0:1]
            m = jnp.maximum(jnp.max(s, axis=-1, keepdims=True), sk)
            p = jnp.exp(s - m)
            denom = jnp.sum(p, axis=-1, keepdims=True) + jnp.exp(sk - m)
            o = jnp.dot(p.astype(BF16), vv, preferred_element_type=F32)
            outs.append(o / denom)
        o_ref[0, :, 128 * pair:128 * (pair + 1)] = jnp.where(lo, outs[0], outs[1]).astype(BF16)


def _window_attn(qa, ka, va, bias_a, sink_b):
    b, s, _ = qa.shape
    nb = s // BLK
    kv = lambda f: pl.BlockSpec((1, BLK, 256), f)
    prev = lambda bi, n: (bi, jnp.maximum(n - 1, 0), 0)
    cur = lambda bi, n: (bi, n, 0)
    nxt = lambda bi, n: (bi, jnp.minimum(n + 1, nb - 1), 0)
    return pl.pallas_call(
        _window_kernel,
        out_shape=jax.ShapeDtypeStruct((b, s, 512), BF16),
        grid=(b, nb),
        in_specs=[pl.BlockSpec((1, BLK, 512), cur), kv(prev), kv(cur), kv(nxt), kv(prev), kv(cur), kv(nxt),
                  pl.BlockSpec((HA, BLK, 3 * BLK), lambda bi, n: (0, 0, 0)),
                  pl.BlockSpec((HA, 128), lambda bi, n: (0, 0))],
        out_specs=pl.BlockSpec((1, BLK, 512), cur),
        compiler_params=_cparams(("parallel", "arbitrary")),
        name="window_attn",
    )(qa, ka, ka, ka, va, va, va, bias_a, sink_b)


ONES_ROWS = BF16_TILE


def _diff_kernel(q_ref, k_ref, vt_ref, bias_ref, lamv_ref, g_ref, o_ref, m_scr, acc_scr, *, lam_init):
    j = pl.program_id(2)
    nk = pl.num_programs(2)
    tk = k_ref.shape[1]

    @pl.when(j == 0)
    def _():
        m_scr[...] = jnp.full(m_scr.shape, -jnp.inf, F32)
        acc_scr[...] = jnp.zeros(acc_scr.shape, F32)

    lane = lax.broadcasted_iota(jnp.int32, (1, 128), 1)
    lo = lane < DH
    zero = jnp.zeros((), BF16)
    ones = jnp.ones((ONES_ROWS, tk), BF16)

    def scores(idx):
        h, c = divmod(idx, 2)
        qh = q_ref[0, :, 128 * h:128 * (h + 1)]
        kh = k_ref[0, :, 128 * h:128 * (h + 1)]
        qm = jnp.where(lo if c == 0 else jnp.logical_not(lo), qh, zero)
        return lax.dot_general(kh, qm, NT_DIMS, preferred_element_type=F32) + bias_ref[0, h]

    s_next = scores(0)
    for idx in range(2 * HB):
        s = s_next
        if idx + 1 < 2 * HB:
            s_next = scores(idx + 1)
        h = idx // 2
        vaug = jnp.concatenate([vt_ref[128 * h:128 * (h + 1), :], ones], axis=0)
        m_old = m_scr[idx]
        m_new = jnp.maximum(m_old, jnp.max(s, axis=0, keepdims=True))
        p = jnp.exp2(s - m_new)
        acc_scr[idx] = acc_scr[idx] * jnp.exp2(m_old - m_new) + jnp.dot(
            vaug, p.astype(BF16), preferred_element_type=F32)
        m_scr[idx] = m_new

    @pl.when(j == nk - 1)
    def _():
        lv = lamv_ref[...]
        dot1 = jnp.sum(lv[0:1] * lv[1:2], axis=-1, keepdims=True)
        dot2 = jnp.sum(lv[2:3] * lv[3:4], axis=-1, keepdims=True)
        lam = jnp.exp(dot1) - jnp.exp(dot2) + lam_init
        for h in range(HB):
            a0 = acc_scr[2 * h]
            a1 = acc_scr[2 * h + 1]
            o = a0[:DVB] / a0[DVB:DVB + 1] - lam * (a1[:DVB] / a1[DVB:DVB + 1])
            o = o * lax.rsqrt(jnp.mean(o * o, axis=0, keepdims=True) + LN_EPS)
            o = jnp.transpose(o, (1, 0)) * g_ref[...] * (1.0 - lam_init)
            o_ref[0, :, 128 * h:128 * (h + 1)] = o.astype(BF16)


def _diff_attn(qb, kb, vbt, bias5, lamv, norm_g, lam_init, t):
    b, s, _ = qb.shape
    nt = s // t
    return pl.pallas_call(
        functools.partial(_diff_kernel, lam_init=lam_init),
        out_shape=jax.ShapeDtypeStruct((b, s, 512), BF16),
        grid=(b, nt, nt),
        in_specs=[pl.BlockSpec((1, t, 512), lambda bi, i, j: (bi, i, 0)),
                  pl.BlockSpec((1, t, 512), lambda bi, i, j: (bi, j, 0)),
                  pl.BlockSpec((HB * DVB, t), lambda bi, i, j: (0, bi * nt + j)),
                  pl.BlockSpec((1, HB, t, t), lambda bi, i, j: (jnp.clip(j - i, -2, 2) + 2, 0, 0, 0)),
                  pl.BlockSpec((4, DH), lambda bi, i, j: (0, 0)),
                  pl.BlockSpec((1, DVB), lambda bi, i, j: (0, 0))],
        out_specs=pl.BlockSpec((1, t, 512), lambda bi, i, j: (bi, i, 0)),
        scratch_shapes=[pltpu.VMEM((2 * HB, 1, t), F32), pltpu.VMEM((2 * HB, DVB + ONES_ROWS, t), F32)],
        compiler_params=_cparams(("parallel", "parallel", "arbitrary")),
        name="diff_attn",
    )(qb, kb, vbt, bias5, lamv, norm_g)


def _mix_kernel(a_ref, b_ref, ga_ref, gb_ref, x_ref, wa_ref, wb_ref, wo_ref, g_ref, bb_ref, o_ref):
    a = jnp.dot(a_ref[...], wa_ref[...], preferred_element_type=F32)
    b = jnp.dot(b_ref[...], wb_ref[...], preferred_element_type=F32)
    mix_in = ga_ref[...].astype(F32) * a + gb_ref[...].astype(F32) * b
    mix = jnp.dot(mix_in.astype(BF16), wo_ref[...], preferred_element_type=F32)
    o_ref[...] = _layer_norm(ALPHA * x_ref[...] + mix, g_ref[...], bb_ref[...])


def _attn_mix(a, b, ga, gb, x, wa, wb, wo, g, bb, tm=512):
    t, d = x.shape
    tile = lambda n: pl.BlockSpec((tm, n), lambda i: (i, 0))
    full = lambda r, c: pl.BlockSpec((r, c), lambda i: (0, 0))
    return pl.pallas_call(
        _mix_kernel,
        out_shape=jax.ShapeDtypeStruct((t, d), F32),
        grid=(t // tm,),
        in_specs=[tile(512), tile(512), tile(d), tile(d), tile(d),
                  full(512, d), full(512, d), full(d, d), full(1, d), full(1, d)],
        out_specs=tile(d),
        compiler_params=_cparams(("parallel",)),
        name="attn_mix",
    )(a, b, ga, gb, x, wa, wb, wo, g.reshape(1, d), bb.reshape(1, d))


N_CAND = 80
BIG_IDX = 1e9


def _cand_tables():
    flat = np.full((N_CAND,), BIG_IDX, np.float32)
    valid = np.zeros((N_CAND,), np.float32)
    for i in range(16):
        flat[i] = i * 16
        valid[i] = 1
    for j in range(1, 8):
        for i in range(8):
            if (i + 1) * (j + 1) <= 16:
                r = 16 + (j - 1) * 8 + i
                flat[r] = i * 16 + j
                valid[r] = 1
    for j in range(8, 16):
        flat[72 + j - 8] = j
        valid[72 + j - 8] = 1
    return flat, valid


def _lex_argmax(v, idx):
    mx = jnp.max(v, axis=0, keepdims=True)
    mi = jnp.min(jnp.where(v == mx, idx, BIG_IDX), axis=0, keepdims=True)
    return mx, mi


def _select_kernel(x_ref, wq_ref, sk_ref, cflat_ref, cvalid_ref,
                   p1_ref, n_ref, p2_ref, r2_ref,
                   q_scr, sc_scr, work_scr, rank_scr, sort_scr, cand_scr, sel_scr):
    tt = x_ref.shape[0]
    xb = x_ref[...].astype(BF16)
    q_scr[...] = lax.dot_general(wq_ref[...], xb, NT_DIMS, preferred_element_type=F32).astype(BF16)
    key_iota = lax.broadcasted_iota(jnp.int32, (N_KEYS, tt), 0).astype(F32)
    neg_inf = jnp.float32(-jnp.inf)

    def extract(exact):
        work_scr[...] = sc_scr[...]
        rank_scr[...] = jnp.full((2, N_KEYS, tt), float(PEER_TOPK), F32)

        def round_body(r, c):
            for p in range(2):
                w = work_scr[p]
                if exact:
                    mx, mi = _lex_argmax(w, key_iota)
                    hit = key_iota == mi
                else:
                    mx = jnp.max(w, axis=0, keepdims=True)
                    hit = w == mx
                work_scr[p] = jnp.where(hit, neg_inf, w)
                rank_scr[p] = jnp.where(hit, jnp.asarray(r, F32), rank_scr[p])
                sort_scr[p, pl.ds(r, 1), :] = mx
            return c

        lax.fori_loop(0, PEER_TOPK, round_body, 0)
        taken = jnp.where(rank_scr[...] < float(PEER_TOPK), 1.0, 0.0)
        return jnp.sum(taken[0], axis=0, keepdims=True), jnp.sum(taken[1], axis=0, keepdims=True)

    def pair_stage(exact):
        s1 = sort_scr[0]
        s2 = sort_scr[1]
        top = s1[0:1] + s2[0:1]
        groups = [s1 + s2[0:1]]
        for j in range(1, 8):
            groups.append(s1[0:8] + s2[j:j + 1])
        groups.append(s1[0:1] + s2[8:16])
        cflat = cflat_ref[...]
        cand_scr[...] = jnp.where(cvalid_ref[...] > 0.5, jnp.concatenate(groups, axis=0), neg_inf)
        sel_scr[...] = jnp.zeros((N_CAND, tt), F32)

        def cand_round(r, z):
            w = cand_scr[...]
            if exact:
                mx, mi = _lex_argmax(w, cflat)
                hit = cflat == mi
            else:
                mx = jnp.max(w, axis=0, keepdims=True)
                hit = w == mx
            cand_scr[...] = jnp.where(hit, neg_inf, w)
            sel_scr[...] = jnp.where(hit, 1.0, sel_scr[...])
            return z + jnp.exp(mx - top)

        z = lax.fori_loop(0, PEER_TOPK, cand_round, jnp.zeros((1, tt), F32))
        return z, jnp.sum(sel_scr[...], axis=0, keepdims=True)

    def finish(h, z):
        s1 = sort_scr[0]
        s2 = sort_scr[1]
        sel = sel_scr[...]
        cnt_lo = sel[0:8]
        for g in range(1, 8):
            cnt_lo = cnt_lo + sel[8 + 8 * g:16 + 8 * g]
        extra = jnp.sum(sel[72:80], axis=0, keepdims=True)
        row0 = lax.broadcasted_iota(jnp.int32, (8, tt), 0) == 0
        cnt_lo = cnt_lo + jnp.where(row0, extra, 0.0)
        cnt = jnp.concatenate([cnt_lo, sel[8:16]], axis=0)

        r1 = rank_scr[0]
        r2 = rank_scr[1]
        nsel = jnp.zeros((N_KEYS, tt), F32)
        for i in range(PEER_TOPK):
            nsel = jnp.where(r1 == float(i), cnt[i:i + 1], nsel)
        in1 = r1 < float(PEER_TOPK)
        in2 = r2 < float(PEER_TOPK)
        p1 = jnp.where(in1, jnp.exp(sc_scr[0] - s1[0:1]), 0.0) * (0.5 / z)
        p2 = jnp.where(in2, jnp.exp(sc_scr[1] - s2[0:1]), 0.0)
        p1_ref[h] = p1
        n_ref[h] = nsel
        p2_ref[h] = p2.astype(BF16).reshape(N_KEYS // BF16_TILE, BF16_TILE, tt)
        r2_ref[h] = r2.astype(BF16).reshape(N_KEYS // BF16_TILE, BF16_TILE, tt)

    def head_body(h, carry):
        for p in range(2):
            hp = 2 * h + p
            off = pl.multiple_of(hp * 128, 128)
            sc_scr[p] = jnp.dot(sk_ref[hp], q_scr[pl.ds(off, 128), :], preferred_element_type=F32)
        t0, t1 = extract(False)
        taken = jnp.abs(t0 - float(PEER_TOPK)) + jnp.abs(t1 - float(PEER_TOPK))
        z, marks = pair_stage(False)
        taken = taken + jnp.abs(marks - float(PEER_TOPK))
        finish(h, z)

        @pl.when(jnp.max(taken) > 0.0)
        def _():
            extract(True)
            finish(h, pair_stage(True)[0])

        return carry

    lax.fori_loop(0, PEER_HEADS, head_body, 0)


def _peer_select(x, wq_t, sk, tt=256):
    t, d = x.shape
    cflat, cvalid = _cand_tables()
    cflat = jnp.asarray(np.broadcast_to(cflat[:, None], (N_CAND, tt)))
    cvalid = jnp.asarray(np.broadcast_to(cvalid[:, None], (N_CAND, tt)))
    out = jax.ShapeDtypeStruct((PEER_HEADS, N_KEYS, t), F32)
    out16 = jax.ShapeDtypeStruct((PEER_HEADS, N_KEYS // BF16_TILE, BF16_TILE, t), BF16)
    ospec = pl.BlockSpec((PEER_HEADS, N_KEYS, tt), lambda i: (0, 0, i))
    ospec16 = pl.BlockSpec((PEER_HEADS, N_KEYS // BF16_TILE, BF16_TILE, tt), lambda i: (0, 0, 0, i))
    return pl.pallas_call(
        _select_kernel,
        out_shape=[out, out, out16, out16],
        grid=(t // tt,),
        in_specs=[pl.BlockSpec((tt, d), lambda i: (i, 0)),
                  pl.BlockSpec((2 * PEER_HEADS * 128, d), lambda i: (0, 0)),
                  pl.BlockSpec((2 * PEER_HEADS, N_KEYS, 128), lambda i: (0, 0, 0)),
                  pl.BlockSpec((N_CAND, tt), lambda i: (0, 0)),
                  pl.BlockSpec((N_CAND, tt), lambda i: (0, 0))],
        out_specs=[ospec, ospec, ospec16, ospec16],
        scratch_shapes=[pltpu.VMEM((2 * PEER_HEADS * 128, tt), BF16),
                        pltpu.VMEM((2, N_KEYS, tt), F32),
                        pltpu.VMEM((2, N_KEYS, tt), F32),
                        pltpu.VMEM((2, N_KEYS, tt), F32),
                        pltpu.VMEM((2, PEER_TOPK, tt), F32),
                        pltpu.VMEM((N_CAND, tt), F32),
                        pltpu.VMEM((N_CAND, tt), F32)],
        compiler_params=_cparams(("parallel",)),
        name="peer_select",
    )(x, wq_t, sk, cflat, cvalid)


SUB_E = 512


def _peer_kernel(x_ref, u_ref, vt_ref, p1_ref, n_ref, p2_ref, r2_ref, o_ref, xb_scr, w_scr, acc_scr):
    e = pl.program_id(1)
    ne = pl.num_programs(1)
    eb = u_ref.shape[0]
    tt = x_ref.shape[0]

    @pl.when(e == 0)
    def _():
        xb_scr[...] = x_ref[...].astype(BF16)
        acc_scr[...] = jnp.zeros(acc_scr.shape, F32)

    xb = xb_scr[...]
    inv_sqrt2 = 1.0 / math.sqrt(2.0)
    for sb in range(eb // SUB_E):
        hsub = lax.dot_general(u_ref[sb * SUB_E:(sb + 1) * SUB_E, :], xb, NT_DIMS,
                               preferred_element_type=F32)
        for ai in range(SUB_E // N_KEYS):
            a = sb * (SUB_E // N_KEYS) + ai
            hh = hsub[ai * N_KEYS:(ai + 1) * N_KEYS]
            gate = None
            for h in range(PEER_HEADS):
                nrow = jnp.broadcast_to(n_ref[h, a:a + 1, :], (BF16_TILE, tt)).astype(BF16)[None]
                prow = jnp.broadcast_to(p1_ref[h, a:a + 1, :], (BF16_TILE, tt)).astype(BF16)[None]
                term = jnp.where(r2_ref[h] < nrow, p2_ref[h], jnp.zeros((), BF16)) * prow
                gate = term if gate is None else gate + term
            act = hh * (1.0 + lax.erf(hh * inv_sqrt2))
            wgt = act.astype(BF16).reshape(N_KEYS // BF16_TILE, BF16_TILE, tt) * gate
            w_scr[a * N_KEYS:(a + 1) * N_KEYS, :] = wgt.reshape(N_KEYS, tt)
    acc_scr[...] += jnp.dot(vt_ref[...], w_scr[...], preferred_element_type=F32)

    @pl.when(e == ne - 1)
    def _():
        o_ref[...] = acc_scr[...]


def _peer_dense(x, u, vt, p1, nsel, p2, r2, tt=512, eb=2048):
    t, d = x.shape
    ra = eb // N_KEYS
    rows = pl.BlockSpec((PEER_HEADS, ra, tt), lambda i, e: (0, e, i))
    full = pl.BlockSpec((PEER_HEADS, N_KEYS // BF16_TILE, BF16_TILE, tt), lambda i, e: (0, 0, 0, i))
    return pl.pallas_call(
        _peer_kernel,
        out_shape=jax.ShapeDtypeStruct((d, t), F32),
        grid=(t // tt, N_EXPERTS // eb),
        in_specs=[pl.BlockSpec((tt, d), lambda i, e: (i, 0)),
                  pl.BlockSpec((eb, d), lambda i, e: (e, 0)),
                  pl.BlockSpec((d, eb), lambda i, e: (0, e)),
                  rows, rows, full, full],
        out_specs=pl.BlockSpec((d, tt), lambda i, e: (0, i)),
        scratch_shapes=[pltpu.VMEM((tt, d), BF16), pltpu.VMEM((eb, tt), BF16), pltpu.VMEM((d, tt), F32)],
        compiler_params=_cparams(("parallel", "arbitrary")),
        name="peer_dense",
    )(x, u, vt, p1, nsel, p2, r2)


def _final_kernel(x_ref, cht_ref, p_ref, wg_ref, wp_ref, g_ref, b_ref, o_ref):
    x = x_ref[...]
    gate = jax.nn.sigmoid(jnp.dot(x.astype(BF16), wg_ref[...], preferred_element_type=F32))
    ple = gate * jnp.dot(p_ref[...].astype(BF16), wp_ref[...], preferred_element_type=F32)
    ch = jnp.transpose(cht_ref[...], (1, 0))
    o_ref[...] = _layer_norm(ALPHA * x + ch + ple, g_ref[...], b_ref[...])


def _final(x, cht, p, wg, wp, g, b, tm=256):
    t, d = x.shape
    full = lambda r, c: pl.BlockSpec((r, c), lambda i: (0, 0))
    return pl.pallas_call(
        _final_kernel,
        out_shape=jax.ShapeDtypeStruct((t, d), F32),
        grid=(t // tm,),
        in_specs=[pl.BlockSpec((tm, d), lambda i: (i, 0)),
                  pl.BlockSpec((d, tm), lambda i: (0, i)),
                  pl.BlockSpec((tm, PLE_DIM), lambda i: (i, 0)),
                  full(d, d), full(PLE_DIM, d), full(1, d), full(1, d)],
        out_specs=pl.BlockSpec((tm, d), lambda i: (i, 0)),
        compiler_params=_cparams(("parallel",)),
        name="ple_ln2",
    )(x, cht, p, wg, wp, g.reshape(1, d), b.reshape(1, d))


def _diff_tile(s):
    return min(512, s)


def _trunk(x, p, prm, layers):
    b, s, d = x.shape
    t = b * s
    x = _embed_ln(x.reshape(t, d), prm["emb_g"], prm["emb_b"])
    for i, lw in enumerate(layers):
        qa, ka, va, qb, kb, vbt, ga, gb = _in_proj(x, lw["w_in"], lw["w_vbt"])
        r3 = lambda m: m.reshape(b, s, m.shape[-1])
        a = _window_attn(r3(qa), r3(ka), r3(va), lw["bias_a"], lw["sink"])
        td = _diff_tile(s)
        bd = _diff_attn(r3(qb), r3(kb), vbt, lw["bias_b"][td], lw["lamv"], lw["norm_g"], lw["lam_init"], td)
        x = _attn_mix(a.reshape(t, 512), bd.reshape(t, 512), ga, gb, x,
                      lw["w_a"], lw["w_b"], lw["w_o"], lw["ln1_g"], lw["ln1_b"])
        p1, nsel, p2, r2 = _peer_select(x, lw["wq_t"], lw["subkeys"])
        cht = _peer_dense(x, lw["u"], lw["v_t"], p1, nsel, p2, r2)
        x = _final(x, cht, p[i].reshape(t, PLE_DIM), lw["ple_gate_w"], lw["ple_w"], lw["ln2_g"], lw["ln2_b"])
    return x.reshape(b, s, d)


def kernel(x_prompt, x_sample, p_prompt, p_sample, emb_ln_g, emb_ln_b, rel_bias, w_in, w_a, w_b, w_o, sink,
           lam_q1, lam_k1, lam_q2, lam_k2, diff_norm_g, ln1_g, ln1_b, peer_wq, peer_subkeys, peer_u, peer_v,
           ln2_g, ln2_b, ple_w, ple_gate_w):
    table_a = rel_bias[:, :HA]
    table_b = rel_bias[:, HA:]
    bias_a = _window_bias(table_a)
    diff_tiles = sorted({_diff_tile(x_prompt.shape[1]), _diff_tile(x_sample.shape[1])})
    bias_b = {t: _diff_bias(table_b, t) for t in diff_tiles}
    layers = []
    for i in range(DEPTH):
        layers.append(dict(
            w_in=_prep_w_in(w_in[i])[0],
            w_vbt=_prep_w_in(w_in[i])[1],
            bias_a=bias_a,
            bias_b=bias_b,
            sink=jnp.broadcast_to(sink[i].astype(F32)[:, None], (HA, 128)),
            lamv=jnp.stack([lam_q1[i], lam_k1[i], lam_q2[i], lam_k2[i]], axis=0).astype(F32),
            norm_g=diff_norm_g[i].astype(F32).reshape(1, DVB),
            lam_init=0.8 - 0.6 * math.exp(-0.3 * i),
            w_a=w_a[i].astype(BF16), w_b=w_b[i].astype(BF16), w_o=w_o[i].astype(BF16),
            ln1_g=ln1_g[i], ln1_b=ln1_b[i],
            wq_t=jnp.transpose(peer_wq[i]).astype(BF16),
            subkeys=peer_subkeys[i].reshape(2 * PEER_HEADS, N_KEYS, 128).astype(BF16),
            u=peer_u[i].astype(BF16),
            v_t=jnp.transpose(peer_v[i]).astype(BF16),
            ple_gate_w=ple_gate_w[i].astype(BF16), ple_w=ple_w[i].astype(BF16),
            ln2_g=ln2_g[i], ln2_b=ln2_b[i],
        ))
    prm = dict(emb_g=emb_ln_g, emb_b=emb_ln_b)
    y_prompt = _trunk(x_prompt, p_prompt, prm, layers)
    y_sample = _trunk(x_sample, p_sample, prm, layers)
    return (y_prompt, y_sample)
```

```python
import functools
import math

import numpy as np
import jax
import jax.numpy as jnp
from jax import lax
from jax.experimental import pallas as pl
from jax.experimental.pallas import tpu as pltpu

F32 = jnp.float32
BF16 = jnp.bfloat16

D_MODEL = 1024
DEPTH = 2
HA, HKV, DH = 8, 2, 64
WINDOW = 128
BLK = 128
HB = 4
DVB = 2 * DH
N_BUCKETS = 32
MAX_DIST = 128
N_KEYS = 128
N_EXPERTS = N_KEYS * N_KEYS
PEER_HEADS = 8
PEER_TOPK = 16
PLE_DIM = 256
ALPHA = (2.0 * DEPTH) ** 0.25
LN_EPS = 1e-5
NEG = -1e30
LOG2E = math.log2(math.e)

VMEM_LIMIT = 56 * 1024 * 1024
BF16_TILE = 16

NT_DIMS = (((1,), (1,)), ((), ()))


def _cparams(sem):
    return pltpu.CompilerParams(dimension_semantics=sem, vmem_limit_bytes=VMEM_LIMIT)


def _layer_norm(x, g, b):
    mu = jnp.mean(x, axis=-1, keepdims=True)
    xc = x - mu
    var = jnp.mean(xc * xc, axis=-1, keepdims=True)
    return xc * lax.rsqrt(var + LN_EPS) * g + b


def _ln_kernel(x_ref, g_ref, b_ref, o_ref):
    o_ref[...] = _layer_norm(x_ref[...], g_ref[...], b_ref[...])


def _embed_ln(x, g, b, tm=512):
    t, d = x.shape
    row = pl.BlockSpec((1, d), lambda i: (0, 0))
    return pl.pallas_call(
        _ln_kernel,
        out_shape=jax.ShapeDtypeStruct((t, d), F32),
        grid=(t // tm,),
        in_specs=[pl.BlockSpec((tm, d), lambda i: (i, 0)), row, row],
        out_specs=pl.BlockSpec((tm, d), lambda i: (i, 0)),
        compiler_params=_cparams(("parallel",)),
        name="embed_ln",
    )(x, g.reshape(1, d), b.reshape(1, d))


_QA0, _KA0, _QB0, _KB0, _GA0, _GB0, _INW = 0, 512, 768, 1280, 1792, 2816, 3840
VAT_ROWS = HKV * DH


def _inproj_kernel(x_ref, w_ref, wvt_ref, qa_ref, ka_ref, vat_ref, qb_ref, kb_ref, vbt_ref, ga_ref, gb_ref):
    x = x_ref[...].astype(BF16)

    def proj(lo, hi):
        return jnp.dot(x, w_ref[:, lo:hi], preferred_element_type=F32)

    qa_ref[...] = proj(_QA0, _KA0).astype(BF16)
    ka_ref[...] = proj(_KA0, _QB0).astype(BF16)
    qb_ref[...] = proj(_QB0, _KB0).astype(BF16)
    kb_ref[...] = proj(_KB0, _GA0).astype(BF16)
    vt = lax.dot_general(wvt_ref[...], x, NT_DIMS, preferred_element_type=F32).astype(BF16)
    vat_ref[...] = vt[:VAT_ROWS]
    vbt_ref[...] = vt[VAT_ROWS:]
    ga_ref[...] = jax.nn.sigmoid(proj(_GA0, _GB0)).astype(BF16)
    gb_ref[...] = jax.nn.sigmoid(proj(_GB0, _INW)).astype(BF16)


def _in_proj(x, w, wvt, tm=512):
    t, d = x.shape
    rows = lambda n: (jax.ShapeDtypeStruct((t, n), BF16), pl.BlockSpec((tm, n), lambda i: (i, 0)))
    cols = lambda n: (jax.ShapeDtypeStruct((n, t), BF16), pl.BlockSpec((n, tm), lambda i: (0, i)))
    outs = [rows(512), rows(256), cols(VAT_ROWS), rows(512), rows(512), cols(HB * DVB), rows(d), rows(d)]
    return pl.pallas_call(
        _inproj_kernel,
        out_shape=[o[0] for o in outs],
        grid=(t // tm,),
        in_specs=[pl.BlockSpec((tm, d), lambda i: (i, 0)), pl.BlockSpec((d, _INW), lambda i: (0, 0)),
                  pl.BlockSpec((VAT_ROWS + HB * DVB, d), lambda i: (0, 0))],
        out_specs=[o[1] for o in outs],
        compiler_params=_cparams(("parallel",)),
        name="in_proj",
    )(x, w, wvt)


def _prep_w_in(w):
    sc = DH ** -0.5
    qa = w[:, 0:512] * sc
    ka = w[:, 512:640]
    va = w[:, 640:768]
    qb = w[:, 768:1280] * (sc * LOG2E)
    kb = w[:, 1280:1792]
    vb = w[:, 1792:2304]
    gates = w[:, 2304:]
    dup = lambda m: jnp.concatenate([m[:, 0:64], m[:, 0:64], m[:, 64:128], m[:, 64:128]], axis=1)
    main = jnp.concatenate([qa, dup(ka), qb, kb, gates], axis=1).astype(BF16)
    return main, jnp.transpose(jnp.concatenate([va, vb], axis=1)).astype(BF16)


def _t5_bucket(rel):
    half = N_BUCKETS // 2
    max_exact = half // 2
    ret = jnp.where(rel > 0, half, 0)
    n = jnp.abs(rel)
    nf = jnp.maximum(n, 1).astype(F32)
    large = max_exact + (jnp.log(nf / max_exact) / math.log(MAX_DIST / max_exact)
                         * (half - max_exact)).astype(jnp.int32)
    large = jnp.minimum(large, half - 1)
    return ret + jnp.where(n < max_exact, n, large)


def _bias_lookup(table, bucket):
    tab = table.astype(F32)
    out = jnp.zeros((tab.shape[1],) + bucket.shape, F32)
    for bk in range(N_BUCKETS):
        out = jnp.where((bucket == bk)[None], tab[bk].reshape((-1,) + (1,) * bucket.ndim), out)
    return out


def _window_bias(table_a):
    kpos = jnp.arange(3 * BLK, dtype=jnp.int32)[:, None]
    qpos = jnp.arange(BLK, dtype=jnp.int32)[None, :]
    rel = kpos - BLK - qpos
    bias = _bias_lookup(table_a, _t5_bucket(rel))
    return jnp.where((jnp.abs(rel) <= WINDOW)[None], bias, NEG)


def _diff_bias(table_b, t):
    kpos = jnp.arange(t, dtype=jnp.int32)[:, None]
    qpos = jnp.arange(t, dtype=jnp.int32)[None, :]
    tiles = []
    for d in (-2, -1, 0, 1, 2):
        rel = d * t + kpos - qpos
        tiles.append(_bias_lookup(table_b, _t5_bucket(rel)) * LOG2E)
    return jnp.stack(tiles, axis=0)


def _window_kernel(q_ref, kp_ref, kc_ref, kn_ref, vp_ref, vc_ref, vn_ref, bias_ref, sink_ref, o_ref):
    n = pl.program_id(1)
    nb = pl.num_programs(1)
    q = q_ref[0]
    k3 = jnp.concatenate([kp_ref[0], kc_ref[0], kn_ref[0]], axis=0)
    v3t = jnp.concatenate([vp_ref[...], vc_ref[...], vn_ref[...]], axis=1)
    lane = lax.broadcasted_iota(jnp.int32, (1, 128), 1)
    lo = lane < DH
    row = lax.broadcasted_iota(jnp.int32, (3 * BLK, BLK), 0)
    off_edge = ((row < BLK) & (n == 0)) | ((row >= 2 * BLK) & (n == nb - 1))
    edge = jnp.where(off_edge, NEG, 0.0).astype(F32)
    zero = jnp.zeros((), BF16)
    kv_of = lambda hd: hd // (HA // HKV)
    scores = []
    for hd in range(HA):
        pair, half = divmod(hd, 2)
        q2 = q[:, 128 * pair:128 * (pair + 1)]
        qm = jnp.where(lo if half == 0 else jnp.logical_not(lo), q2, zero)
        kk = k3[:, 128 * kv_of(hd):128 * (kv_of(hd) + 1)]
        s = lax.dot_general(kk, qm, NT_DIMS, preferred_element_type=F32)
        scores.append(s + bias_ref[hd] + edge)
    probs, denoms = [], []
    for hd in range(HA):
        s = scores[hd]
        sk = sink_ref[hd:hd + 1, 0:1]
        m = jnp.maximum(jnp.max(s, axis=0, keepdims=True), sk)
        p = jnp.exp(s - m)
        denoms.append(jnp.sum(p, axis=0, keepdims=True) + jnp.exp(sk - m))
        probs.append(p.astype(BF16))
    outs = []
    for hd in range(HA):
        vv = v3t[DH * kv_of(hd):DH * (kv_of(hd) + 1), :]
        o = jnp.dot(vv, probs[hd], preferred_element_type=F32)
        outs.append(o / denoms[hd])
    for pair in range(HA // 2):
        o2 = jnp.concatenate([outs[2 * pair], outs[2 * pair + 1]], axis=0)
        o_ref[0, :, 128 * pair:128 * (pair + 1)] = jnp.transpose(o2, (1, 0)).astype(BF16)


def _window_attn(qa, ka, vat, bias_a, sink_b):
    b, s, _ = qa.shape
    nb = s // BLK
    kv = lambda f: pl.BlockSpec((1, BLK, 256), f)
    prev = lambda bi, n: (bi, jnp.maximum(n - 1, 0), 0)
    cur = lambda bi, n: (bi, n, 0)
    nxt = lambda bi, n: (bi, jnp.minimum(n + 1, nb - 1), 0)
    vt = lambda f: pl.BlockSpec((VAT_ROWS, BLK), lambda bi, n: (0, bi * nb + f(bi, n)[1]))
    return pl.pallas_call(
        _window_kernel,
        out_shape=jax.ShapeDtypeStruct((b, s, 512), BF16),
        grid=(b, nb),
        in_specs=[pl.BlockSpec((1, BLK, 512), cur), kv(prev), kv(cur), kv(nxt), vt(prev), vt(cur), vt(nxt),
                  pl.BlockSpec((HA, 3 * BLK, BLK), lambda bi, n: (0, 0, 0)),
                  pl.BlockSpec((HA, 128), lambda bi, n: (0, 0))],
        out_specs=pl.BlockSpec((1, BLK, 512), cur),
        compiler_params=_cparams(("parallel", "arbitrary")),
        name="window_attn",
    )(qa, ka, ka, ka, vat, vat, vat, bias_a, sink_b)


ONES_ROWS = BF16_TILE


def _diff_kernel(q_ref, k_ref, vt_ref, bias_ref, lamv_ref, g_ref, o_ref, m_scr, acc_scr, *, lam_init):
    j = pl.program_id(2)
    nk = pl.num_programs(2)
    tk = k_ref.shape[1]

    @pl.when(j == 0)
    def _():
        m_scr[...] = jnp.full(m_scr.shape, -jnp.inf, F32)
        acc_scr[...] = jnp.zeros(acc_scr.shape, F32)

    lane = lax.broadcasted_iota(jnp.int32, (1, 128), 1)
    lo = lane < DH
    zero = jnp.zeros((), BF16)
    ones = jnp.ones((ONES_ROWS, tk), BF16)

    def scores(idx):
        h, c = divmod(idx, 2)
        qh = q_ref[0, :, 128 * h:128 * (h + 1)]
        kh = k_ref[0, :, 128 * h:128 * (h + 1)]
        qm = jnp.where(lo if c == 0 else jnp.logical_not(lo), qh, zero)
        return lax.dot_general(kh, qm, NT_DIMS, preferred_element_type=F32) + bias_ref[0, h]

    def values(idx, p, alpha):
        h = idx // 2
        vaug = jnp.concatenate([vt_ref[128 * h:128 * (h + 1), :], ones], axis=0)
        acc_scr[idx] = acc_scr[idx] * alpha + jnp.dot(vaug, p, preferred_element_type=F32)

    s_next = scores(0)
    pending = None
    for idx in range(2 * HB):
        s = s_next
        if idx + 1 < 2 * HB:
            s_next = scores(idx + 1)
        m_old = m_scr[idx]
        m_new = jnp.maximum(m_old, jnp.max(s, axis=0, keepdims=True))
        p = jnp.exp2(s - m_new).astype(BF16)
        alpha = jnp.exp2(m_old - m_new)
        m_scr[idx] = m_new
        if pending is not None:
            values(*pending)
        pending = (idx, p, alpha)
    values(*pending)

    @pl.when(j == nk - 1)
    def _():
        lv = lamv_ref[...]
        dot1 = jnp.sum(lv[0:1] * lv[1:2], axis=-1, keepdims=True)
        dot2 = jnp.sum(lv[2:3] * lv[3:4], axis=-1, keepdims=True)
        lam = jnp.exp(dot1) - jnp.exp(dot2) + lam_init
        for h in range(HB):
            a0 = acc_scr[2 * h]
            a1 = acc_scr[2 * h + 1]
            o = a0[:DVB] / a0[DVB:DVB + 1] - lam * (a1[:DVB] / a1[DVB:DVB + 1])
            o = o * lax.rsqrt(jnp.mean(o * o, axis=0, keepdims=True) + LN_EPS)
            o = jnp.transpose(o, (1, 0)) * g_ref[...] * (1.0 - lam_init)
            o_ref[0, :, 128 * h:128 * (h + 1)] = o.astype(BF16)


def _diff_attn(qb, kb, vbt, bias5, lamv, norm_g, lam_init, t):
    b, s, _ = qb.shape
    nt = s // t
    return pl.pallas_call(
        functools.partial(_diff_kernel, lam_init=lam_init),
        out_shape=jax.ShapeDtypeStruct((b, s, 512), BF16),
        grid=(b, nt, nt),
        in_specs=[pl.BlockSpec((1, t, 512), lambda bi, i, j: (bi, i, 0)),
                  pl.BlockSpec((1, t, 512), lambda bi, i, j: (bi, j, 0)),
                  pl.BlockSpec((HB * DVB, t), lambda bi, i, j: (0, bi * nt + j)),
                  pl.BlockSpec((1, HB, t, t), lambda bi, i, j: (jnp.clip(j - i, -2, 2) + 2, 0, 0, 0)),
                  pl.BlockSpec((4, DH), lambda bi, i, j: (0, 0)),
                  pl.BlockSpec((1, DVB), lambda bi, i, j: (0, 0))],
        out_specs=pl.BlockSpec((1, t, 512), lambda bi, i, j: (bi, i, 0)),
        scratch_shapes=[pltpu.VMEM((2 * HB, 1, t), F32), pltpu.VMEM((2 * HB, DVB + ONES_ROWS, t), F32)],
        compiler_params=_cparams(("parallel", "parallel", "arbitrary")),
        name="diff_attn",
    )(qb, kb, vbt, bias5, lamv, norm_g)


def _mix_kernel(a_ref, b_ref, ga_ref, gb_ref, x_ref, wa_ref, wb_ref, wo_ref, g_ref, bb_ref, o_ref):
    a = jnp.dot(a_ref[...], wa_ref[...], preferred_element_type=F32)
    b = jnp.dot(b_ref[...], wb_ref[...], preferred_element_type=F32)
    mix_in = ga_ref[...].astype(F32) * a + gb_ref[...].astype(F32) * b
    mix = jnp.dot(mix_in.astype(BF16), wo_ref[...], preferred_element_type=F32)
    o_ref[...] = _layer_norm(ALPHA * x_ref[...] + mix, g_ref[...], bb_ref[...])


def _attn_mix(a, b, ga, gb, x, wa, wb, wo, g, bb, tm=512):
    t, d = x.shape
    tile = lambda n: pl.BlockSpec((tm, n), lambda i: (i, 0))
    full = lambda r, c: pl.BlockSpec((r, c), lambda i: (0, 0))
    return pl.pallas_call(
        _mix_kernel,
        out_shape=jax.ShapeDtypeStruct((t, d), F32),
        grid=(t // tm,),
        in_specs=[tile(512), tile(512), tile(d), tile(d), tile(d),
                  full(512, d), full(512, d), full(d, d), full(1, d), full(1, d)],
        out_specs=tile(d),
        compiler_params=_cparams(("parallel",)),
        name="attn_mix",
    )(a, b, ga, gb, x, wa, wb, wo, g.reshape(1, d), bb.reshape(1, d))


N_CAND = 80
BIG_IDX = 1e9


def _cand_tables():
    flat = np.full((N_CAND,), BIG_IDX, np.float32)
    valid = np.zeros((N_CAND,), np.float32)
    for i in range(16):
        flat[i] = i * 16
        valid[i] = 1
    for j in range(1, 8):
        for i in range(8):
            if (i + 1) * (j + 1) <= 16:
                r = 16 + (j - 1) * 8 + i
                flat[r] = i * 16 + j
                valid[r] = 1
    for j in range(8, 16):
        flat[72 + j - 8] = j
        valid[72 + j - 8] = 1
    return flat, valid


def _lex_argmax(v, idx):
    mx = jnp.max(v, axis=0, keepdims=True)
    mi = jnp.min(jnp.where(v == mx, idx, BIG_IDX), axis=0, keepdims=True)
    return mx, mi


def _select_kernel(x_ref, wq_ref, sk_ref, cflat_ref, cvalid_ref,
                   p1_ref, n_ref, p2_ref, r2_ref,
                   q_scr, sc_scr, work_scr, rank_scr, sort_scr, cand_scr, sel_scr):
    tt = x_ref.shape[0]
    xb = x_ref[...].astype(BF16)
    q_scr[...] = lax.dot_general(wq_ref[...], xb, NT_DIMS, preferred_element_type=F32).astype(BF16)
    key_iota = lax.broadcasted_iota(jnp.int32, (N_KEYS, tt), 0).astype(F32)
    neg_inf = jnp.float32(-jnp.inf)

    def extract(exact):
        work_scr[...] = sc_scr[...]
        rank_scr[...] = jnp.full((2, N_KEYS, tt), float(PEER_TOPK), F32)

        def round_body(r, c):
            for p in range(2):
                w = work_scr[p]
                if exact:
                    mx, mi = _lex_argmax(w, key_iota)
                    hit = key_iota == mi
                else:
                    mx = jnp.max(w, axis=0, keepdims=True)
                    hit = w == mx
                work_scr[p] = jnp.where(hit, neg_inf, w)
                rank_scr[p] = jnp.where(hit, jnp.asarray(r, F32), rank_scr[p])
                sort_scr[p, pl.ds(r, 1), :] = mx
            return c

        lax.fori_loop(0, PEER_TOPK, round_body, 0)
        taken = jnp.where(rank_scr[...] < float(PEER_TOPK), 1.0, 0.0)
        return jnp.sum(taken[0], axis=0, keepdims=True), jnp.sum(taken[1], axis=0, keepdims=True)

    def pair_stage(exact):
        s1 = sort_scr[0]
        s2 = sort_scr[1]
        top = s1[0:1] + s2[0:1]
        groups = [s1 + s2[0:1]]
        for j in range(1, 8):
            groups.append(s1[0:8] + s2[j:j + 1])
        groups.append(s1[0:1] + s2[8:16])
        cflat = cflat_ref[...]
        cand_scr[...] = jnp.where(cvalid_ref[...] > 0.5, jnp.concatenate(groups, axis=0), neg_inf)
        sel_scr[...] = jnp.zeros((N_CAND, tt), F32)

        def cand_round(r, z):
            w = cand_scr[...]
            if exact:
                mx, mi = _lex_argmax(w, cflat)
                hit = cflat == mi
            else:
                mx = jnp.max(w, axis=0, keepdims=True)
                hit = w == mx
            cand_scr[...] = jnp.where(hit, neg_inf, w)
            sel_scr[...] = jnp.where(hit, 1.0, sel_scr[...])
            return z + jnp.exp(mx - top)

        z = lax.fori_loop(0, PEER_TOPK, cand_round, jnp.zeros((1, tt), F32))
        return z, jnp.sum(sel_scr[...], axis=0, keepdims=True)

    def finish(h, z):
        s1 = sort_scr[0]
        s2 = sort_scr[1]
        sel = sel_scr[...]
        cnt_lo = sel[0:8]
        for g in range(1, 8):
            cnt_lo = cnt_lo + sel[8 + 8 * g:16 + 8 * g]
        extra = jnp.sum(sel[72:80], axis=0, keepdims=True)
        row0 = lax.broadcasted_iota(jnp.int32, (8, tt), 0) == 0
        cnt_lo = cnt_lo + jnp.where(row0, extra, 0.0)
        cnt = jnp.concatenate([cnt_lo, sel[8:16]], axis=0)

        r1 = rank_scr[0]
        r2 = rank_scr[1]
        nsel = jnp.zeros((N_KEYS, tt), F32)
        for i in range(PEER_TOPK):
            nsel = jnp.where(r1 == float(i), cnt[i:i + 1], nsel)
        in1 = r1 < float(PEER_TOPK)
        in2 = r2 < float(PEER_TOPK)
        p1 = jnp.where(in1, jnp.exp(sc_scr[0] - s1[0:1]), 0.0) * (0.5 / z)
        p2 = jnp.where(in2, jnp.exp(sc_scr[1] - s2[0:1]), 0.0)
        p1_ref[h] = p1
        n_ref[h] = nsel
        p2_ref[h] = p2.astype(BF16).reshape(N_KEYS // BF16_TILE, BF16_TILE, tt)
        r2_ref[h] = r2.astype(BF16).reshape(N_KEYS // BF16_TILE, BF16_TILE, tt)

    def head_body(h, carry):
        for p in range(2):
            hp = 2 * h + p
            off = pl.multiple_of(hp * 128, 128)
            sc_scr[p] = jnp.dot(sk_ref[hp], q_scr[pl.ds(off, 128), :], preferred_element_type=F32)
        t0, t1 = extract(False)
        taken = jnp.abs(t0 - float(PEER_TOPK)) + jnp.abs(t1 - float(PEER_TOPK))
        z, marks = pair_stage(False)
        taken = taken + jnp.abs(marks - float(PEER_TOPK))
        finish(h, z)

        @pl.when(jnp.max(taken) > 0.0)
        def _():
            extract(True)
            finish(h, pair_stage(True)[0])

        return carry

    lax.fori_loop(0, PEER_HEADS, head_body, 0)


def _peer_select(x, wq_t, sk, tt=256):
    t, d = x.shape
    cflat, cvalid = _cand_tables()
    cflat = jnp.asarray(np.broadcast_to(cflat[:, None], (N_CAND, tt)))
    cvalid = jnp.asarray(np.broadcast_to(cvalid[:, None], (N_CAND, tt)))
    out = jax.ShapeDtypeStruct((PEER_HEADS, N_KEYS, t), F32)
    out16 = jax.ShapeDtypeStruct((PEER_HEADS, N_KEYS // BF16_TILE, BF16_TILE, t), BF16)
    ospec = pl.BlockSpec((PEER_HEADS, N_KEYS, tt), lambda i: (0, 0, i))
    ospec16 = pl.BlockSpec((PEER_HEADS, N_KEYS // BF16_TILE, BF16_TILE, tt), lambda i: (0, 0, 0, i))
    return pl.pallas_call(
        _select_kernel,
        out_shape=[out, out, out16, out16],
        grid=(t // tt,),
        in_specs=[pl.BlockSpec((tt, d), lambda i: (i, 0)),
                  pl.BlockSpec((2 * PEER_HEADS * 128, d), lambda i: (0, 0)),
                  pl.BlockSpec((2 * PEER_HEADS, N_KEYS, 128), lambda i: (0, 0, 0)),
                  pl.BlockSpec((N_CAND, tt), lambda i: (0, 0)),
                  pl.BlockSpec((N_CAND, tt), lambda i: (0, 0))],
        out_specs=[ospec, ospec, ospec16, ospec16],
        scratch_shapes=[pltpu.VMEM((2 * PEER_HEADS * 128, tt), BF16),
                        pltpu.VMEM((2, N_KEYS, tt), F32),
                        pltpu.VMEM((2, N_KEYS, tt), F32),
                        pltpu.VMEM((2, N_KEYS, tt), F32),
                        pltpu.VMEM((2, PEER_TOPK, tt), F32),
                        pltpu.VMEM((N_CAND, tt), F32),
                        pltpu.VMEM((N_CAND, tt), F32)],
        compiler_params=_cparams(("parallel",)),
        name="peer_select",
    )(x, wq_t, sk, cflat, cvalid)


SUB_E = 256


def _peer_kernel(x_ref, u_ref, vt_ref, p1_ref, n_ref, p2_ref, r2_ref, o_ref, xb_scr, w_scr, acc_scr):
    e = pl.program_id(1)
    ne = pl.num_programs(1)
    eb = u_ref.shape[0]
    tt = x_ref.shape[0]

    @pl.when(e == 0)
    def _():
        xb_scr[...] = x_ref[...].astype(BF16)
        acc_scr[...] = jnp.zeros(acc_scr.shape, F32)

    xb = xb_scr[...]
    inv_sqrt2 = 1.0 / math.sqrt(2.0)
    for sb in range(eb // SUB_E):
        hsub = lax.dot_general(u_ref[sb * SUB_E:(sb + 1) * SUB_E, :], xb, NT_DIMS,
                               preferred_element_type=F32)
        for ai in range(SUB_E // N_KEYS):
            a = sb * (SUB_E // N_KEYS) + ai
            hh = hsub[ai * N_KEYS:(ai + 1) * N_KEYS]
            gate = None
            for h in range(PEER_HEADS):
                nrow = jnp.broadcast_to(n_ref[h, a:a + 1, :], (BF16_TILE, tt)).astype(BF16)[None]
                prow = jnp.broadcast_to(p1_ref[h, a:a + 1, :], (BF16_TILE, tt)).astype(BF16)[None]
                term = jnp.where(r2_ref[h] < nrow, p2_ref[h], jnp.zeros((), BF16)) * prow
                gate = term if gate is None else gate + term
            act = hh * (1.0 + lax.erf(hh * inv_sqrt2))
            wgt = act.astype(BF16).reshape(N_KEYS // BF16_TILE, BF16_TILE, tt) * gate
            w_scr[a * N_KEYS:(a + 1) * N_KEYS, :] = wgt.reshape(N_KEYS, tt)
    acc_scr[...] += jnp.dot(vt_ref[...], w_scr[...], preferred_element_type=F32)

    @pl.when(e == ne - 1)
    def _():
        o_ref[...] = acc_scr[...]


def _peer_dense(x, u, vt, p1, nsel, p2, r2, tt=512, eb=2048):
    t, d = x.shape
    ra = eb // N_KEYS
    rows = pl.BlockSpec((PEER_HEADS, ra, tt), lambda i, e: (0, e, i))
    full = pl.BlockSpec((PEER_HEADS, N_KEYS // BF16_TILE, BF16_TILE, tt), lambda i, e: (0, 0, 0, i))
    return pl.pallas_call(
        _peer_kernel,
        out_shape=jax.ShapeDtypeStruct((d, t), F32),
        grid=(t // tt, N_EXPERTS // eb),
        in_specs=[pl.BlockSpec((tt, d), lambda i, e: (i, 0)),
                  pl.BlockSpec((eb, d), lambda i, e: (e, 0)),
                  pl.BlockSpec((d, eb), lambda i, e: (0, e)),
                  rows, rows, full, full],
        out_specs=pl.BlockSpec((d, tt), lambda i, e: (0, i)),
        scratch_shapes=[pltpu.VMEM((tt, d), BF16), pltpu.VMEM((eb, tt), BF16), pltpu.VMEM((d, tt), F32)],
        compiler_params=_cparams(("parallel", "arbitrary")),
        name="peer_dense",
    )(x, u, vt, p1, nsel, p2, r2)


def _final_kernel(x_ref, cht_ref, p_ref, wg_ref, wp_ref, g_ref, b_ref, o_ref):
    x = x_ref[...]
    gate = jax.nn.sigmoid(jnp.dot(x.astype(BF16), wg_ref[...], preferred_element_type=F32))
    ple = gate * jnp.dot(p_ref[...].astype(BF16), wp_ref[...], preferred_element_type=F32)
    ch = jnp.transpose(cht_ref[...], (1, 0))
    o_ref[...] = _layer_norm(ALPHA * x + ch + ple, g_ref[...], b_ref[...])


def _final(x, cht, p, wg, wp, g, b, tm=256):
    t, d = x.shape
    full = lambda r, c: pl.BlockSpec((r, c), lambda i: (0, 0))
    return pl.pallas_call(
        _final_kernel,
        out_shape=jax.ShapeDtypeStruct((t, d), F32),
        grid=(t // tm,),
        in_specs=[pl.BlockSpec((tm, d), lambda i: (i, 0)),
                  pl.BlockSpec((d, tm), lambda i: (0, i)),
                  pl.BlockSpec((tm, PLE_DIM), lambda i: (i, 0)),
                  full(d, d), full(PLE_DIM, d), full(1, d), full(1, d)],
        out_specs=pl.BlockSpec((tm, d), lambda i: (i, 0)),
        compiler_params=_cparams(("parallel",)),
        name="ple_ln2",
    )(x, cht, p, wg, wp, g.reshape(1, d), b.reshape(1, d))


def _diff_tile(s):
    return min(512, s)


def _trunk(x, p, prm, layers):
    b, s, d = x.shape
    t = b * s
    x = _embed_ln(x.reshape(t, d), prm["emb_g"], prm["emb_b"])
    for i, lw in enumerate(layers):
        qa, ka, vat, qb, kb, vbt, ga, gb = _in_proj(x, lw["w_in"], lw["w_vbt"])
        r3 = lambda m: m.reshape(b, s, m.shape[-1])
        a = _window_attn(r3(qa), r3(ka), vat, lw["bias_a"], lw["sink"])
        td = _diff_tile(s)
        bd = _diff_attn(r3(qb), r3(kb), vbt, lw["bias_b"][td], lw["lamv"], lw["norm_g"], lw["lam_init"], td)
        x = _attn_mix(a.reshape(t, 512), bd.reshape(t, 512), ga, gb, x,
                      lw["w_a"], lw["w_b"], lw["w_o"], lw["ln1_g"], lw["ln1_b"])
        p1, nsel, p2, r2 = _peer_select(x, lw["wq_t"], lw["subkeys"])
        cht = _peer_dense(x, lw["u"], lw["v_t"], p1, nsel, p2, r2)
        x = _final(x, cht, p[i].reshape(t, PLE_DIM), lw["ple_gate_w"], lw["ple_w"], lw["ln2_g"], lw["ln2_b"])
    return x.reshape(b, s, d)


def kernel(x_prompt, x_sample, p_prompt, p_sample, emb_ln_g, emb_ln_b, rel_bias, w_in, w_a, w_b, w_o, sink,
           lam_q1, lam_k1, lam_q2, lam_k2, diff_norm_g, ln1_g, ln1_b, peer_wq, peer_subkeys, peer_u, peer_v,
           ln2_g, ln2_b, ple_w, ple_gate_w):
    table_a = rel_bias[:, :HA]
    table_b = rel_bias[:, HA:]
    bias_a = _window_bias(table_a)
    diff_tiles = sorted({_diff_tile(x_prompt.shape[1]), _diff_tile(x_sample.shape[1])})
    bias_b = {t: _diff_bias(table_b, t) for t in diff_tiles}
    layers = []
    for i in range(DEPTH):
        layers.append(dict(
            w_in=_prep_w_in(w_in[i])[0],
            w_vbt=_prep_w_in(w_in[i])[1],
            bias_a=bias_a,
            bias_b=bias_b,
            sink=jnp.broadcast_to(sink[i].astype(F32)[:, None], (HA, 128)),
            lamv=jnp.stack([lam_q1[i], lam_k1[i], lam_q2[i], lam_k2[i]], axis=0).astype(F32),
            norm_g=diff_norm_g[i].astype(F32).reshape(1, DVB),
            lam_init=0.8 - 0.6 * math.exp(-0.3 * i),
            w_a=w_a[i].astype(BF16), w_b=w_b[i].astype(BF16), w_o=w_o[i].astype(BF16),
            ln1_g=ln1_g[i], ln1_b=ln1_b[i],
            wq_t=jnp.transpose(peer_wq[i]).astype(BF16),
            subkeys=peer_subkeys[i].reshape(2 * PEER_HEADS, N_KEYS, 128).astype(BF16),
            u=peer_u[i].astype(BF16),
            v_t=jnp.transpose(peer_v[i]).astype(BF16),
            ple_gate_w=ple_gate_w[i].astype(BF16), ple_w=ple_w[i].astype(BF16),
            ln2_g=ln2_g[i], ln2_b=ln2_b[i],
        ))
    prm = dict(emb_g=emb_ln_g, emb_b=emb_ln_b)
    y_prompt = _trunk(x_prompt, p_prompt, prm, layers)
    y_sample = _trunk(x_sample, p_sample, prm, layers)
    return (y_prompt, y_sample)
```

```python
import functools
import math

import numpy as np
import jax
import jax.numpy as jnp
from jax import lax
from jax.experimental import pallas as pl
from jax.experimental.pallas import tpu as pltpu

F32 = jnp.float32
BF16 = jnp.bfloat16

D_MODEL = 1024
DEPTH = 2
HA, HKV, DH = 8, 2, 64
WINDOW = 128
BLK = 128
HB = 4
DVB = 2 * DH
N_BUCKETS = 32
MAX_DIST = 128
N_KEYS = 128
N_EXPERTS = N_KEYS * N_KEYS
PEER_HEADS = 8
PEER_TOPK = 16
PLE_DIM = 256
ALPHA = (2.0 * DEPTH) ** 0.25
LN_EPS = 1e-5
NEG = -1e30
LOG2E = math.log2(math.e)

VMEM_LIMIT = 56 * 1024 * 1024
BF16_TILE = 16

NT_DIMS = (((1,), (1,)), ((), ()))


def _cparams(sem):
    return pltpu.CompilerParams(dimension_semantics=sem, vmem_limit_bytes=VMEM_LIMIT)


def _layer_norm(x, g, b):
    mu = jnp.mean(x, axis=-1, keepdims=True)
    xc = x - mu
    var = jnp.mean(xc * xc, axis=-1, keepdims=True)
    return xc * lax.rsqrt(var + LN_EPS) * g + b


def _ln_kernel(x_ref, g_ref, b_ref, o_ref):
    o_ref[...] = _layer_norm(x_ref[...], g_ref[...], b_ref[...])


def _embed_ln(x, g, b, tm=512):
    t, d = x.shape
    row = pl.BlockSpec((1, d), lambda i: (0, 0))
    return pl.pallas_call(
        _ln_kernel,
        out_shape=jax.ShapeDtypeStruct((t, d), F32),
        grid=(t // tm,),
        in_specs=[pl.BlockSpec((tm, d), lambda i: (i, 0)), row, row],
        out_specs=pl.BlockSpec((tm, d), lambda i: (i, 0)),
        compiler_params=_cparams(("parallel",)),
        name="embed_ln",
    )(x, g.reshape(1, d), b.reshape(1, d))


_QA0, _KA0, _QB0, _KB0, _GA0, _GB0, _INW = 0, 512, 768, 1280, 1792, 2816, 3840
VAT_ROWS = HKV * DH


def _inproj_kernel(x_ref, w_ref, wvt_ref, qa_ref, ka_ref, vat_ref, qb_ref, kb_ref, vbt_ref, ga_ref, gb_ref):
    x = x_ref[...].astype(BF16)

    def proj(lo, hi):
        return jnp.dot(x, w_ref[:, lo:hi], preferred_element_type=F32)

    qa_ref[...] = proj(_QA0, _KA0).astype(BF16)
    ka_ref[...] = proj(_KA0, _QB0).astype(BF16)
    qb_ref[...] = proj(_QB0, _KB0).astype(BF16)
    kb_ref[...] = proj(_KB0, _GA0).astype(BF16)
    vt = lax.dot_general(wvt_ref[...], x, NT_DIMS, preferred_element_type=F32).astype(BF16)
    vat_ref[...] = vt[:VAT_ROWS]
    vbt_ref[...] = vt[VAT_ROWS:]
    ga_ref[...] = jax.nn.sigmoid(proj(_GA0, _GB0)).astype(BF16)
    gb_ref[...] = jax.nn.sigmoid(proj(_GB0, _INW)).astype(BF16)


def _in_proj(x, w, wvt, tm=512):
    t, d = x.shape
    rows = lambda n: (jax.ShapeDtypeStruct((t, n), BF16), pl.BlockSpec((tm, n), lambda i: (i, 0)))
    cols = lambda n: (jax.ShapeDtypeStruct((n, t), BF16), pl.BlockSpec((n, tm), lambda i: (0, i)))
    outs = [rows(512), rows(256), cols(VAT_ROWS), rows(512), rows(512), cols(HB * DVB), rows(d), rows(d)]
    return pl.pallas_call(
        _inproj_kernel,
        out_shape=[o[0] for o in outs],
        grid=(t // tm,),
        in_specs=[pl.BlockSpec((tm, d), lambda i: (i, 0)), pl.BlockSpec((d, _INW), lambda i: (0, 0)),
                  pl.BlockSpec((VAT_ROWS + HB * DVB, d), lambda i: (0, 0))],
        out_specs=[o[1] for o in outs],
        compiler_params=_cparams(("parallel",)),
        name="in_proj",
    )(x, w, wvt)


def _prep_w_in(w):
    sc = DH ** -0.5
    qa = w[:, 0:512] * sc
    ka = w[:, 512:640]
    va = w[:, 640:768]
    qb = w[:, 768:1280] * (sc * LOG2E)
    kb = w[:, 1280:1792]
    vb = w[:, 1792:2304]
    gates = w[:, 2304:]
    dup = lambda m: jnp.concatenate([m[:, 0:64], m[:, 0:64], m[:, 64:128], m[:, 64:128]], axis=1)
    main = jnp.concatenate([qa, dup(ka), qb, kb, gates], axis=1).astype(BF16)
    return main, jnp.transpose(jnp.concatenate([va, vb], axis=1)).astype(BF16)


def _t5_bucket(rel):
    half = N_BUCKETS // 2
    max_exact = half // 2
    ret = jnp.where(rel > 0, half, 0)
    n = jnp.abs(rel)
    nf = jnp.maximum(n, 1).astype(F32)
    large = max_exact + (jnp.log(nf / max_exact) / math.log(MAX_DIST / max_exact)
                         * (half - max_exact)).astype(jnp.int32)
    large = jnp.minimum(large, half - 1)
    return ret + jnp.where(n < max_exact, n, large)


def _bias_lookup(table, bucket):
    tab = table.astype(F32)
    out = jnp.zeros((tab.shape[1],) + bucket.shape, F32)
    for bk in range(N_BUCKETS):
        out = jnp.where((bucket == bk)[None], tab[bk].reshape((-1,) + (1,) * bucket.ndim), out)
    return out


def _window_bias(table_a):
    kpos = jnp.arange(3 * BLK, dtype=jnp.int32)[:, None]
    qpos = jnp.arange(BLK, dtype=jnp.int32)[None, :]
    rel = kpos - BLK - qpos
    bias = _bias_lookup(table_a, _t5_bucket(rel))
    return jnp.where((jnp.abs(rel) <= WINDOW)[None], bias, NEG)


def _diff_bias(table_b, t):
    kpos = jnp.arange(t, dtype=jnp.int32)[:, None]
    qpos = jnp.arange(t, dtype=jnp.int32)[None, :]
    tiles = []
    for d in (-2, -1, 0, 1, 2):
        rel = d * t + kpos - qpos
        tiles.append(_bias_lookup(table_b, _t5_bucket(rel)) * LOG2E)
    return jnp.stack(tiles, axis=0)


def _window_kernel(q_ref, kp_ref, kc_ref, kn_ref, vp_ref, vc_ref, vn_ref, bias_ref, sink_ref, o_ref):
    n = pl.program_id(1)
    nb = pl.num_programs(1)
    q = q_ref[0]
    k3 = jnp.concatenate([kp_ref[0], kc_ref[0], kn_ref[0]], axis=0)
    v3t = jnp.concatenate([vp_ref[...], vc_ref[...], vn_ref[...]], axis=1)
    lane = lax.broadcasted_iota(jnp.int32, (1, 128), 1)
    lo = lane < DH
    row = lax.broadcasted_iota(jnp.int32, (3 * BLK, BLK), 0)
    off_edge = ((row < BLK) & (n == 0)) | ((row >= 2 * BLK) & (n == nb - 1))
    edge = jnp.where(off_edge, NEG, 0.0).astype(F32)
    zero = jnp.zeros((), BF16)
    kv_of = lambda hd: hd // (HA // HKV)
    scores = []
    for hd in range(HA):
        pair, half = divmod(hd, 2)
        q2 = q[:, 128 * pair:128 * (pair + 1)]
        qm = jnp.where(lo if half == 0 else jnp.logical_not(lo), q2, zero)
        kk = k3[:, 128 * kv_of(hd):128 * (kv_of(hd) + 1)]
        s = lax.dot_general(kk, qm, NT_DIMS, preferred_element_type=F32)
        scores.append(s + bias_ref[hd] + edge)
    probs, denoms = [], []
    for hd in range(HA):
        s = scores[hd]
        sk = sink_ref[hd:hd + 1, 0:1]
        m = jnp.maximum(jnp.max(s, axis=0, keepdims=True), sk)
        p = jnp.exp(s - m)
        denoms.append(jnp.sum(p, axis=0, keepdims=True) + jnp.exp(sk - m))
        probs.append(p.astype(BF16))
    outs = []
    for hd in range(HA):
        vv = v3t[DH * kv_of(hd):DH * (kv_of(hd) + 1), :]
        o = jnp.dot(vv, probs[hd], preferred_element_type=F32)
        outs.append(o / denoms[hd])
    for pair in range(HA // 2):
        o2 = jnp.concatenate([outs[2 * pair], outs[2 * pair + 1]], axis=0)
        o_ref[0, :, 128 * pair:128 * (pair + 1)] = jnp.transpose(o2, (1, 0)).astype(BF16)


def _window_attn(qa, ka, vat, bias_a, sink_b):
    b, s, _ = qa.shape
    nb = s // BLK
    kv = lambda f: pl.BlockSpec((1, BLK, 256), f)
    prev = lambda bi, n: (bi, jnp.maximum(n - 1, 0), 0)
    cur = lambda bi, n: (bi, n, 0)
    nxt = lambda bi, n: (bi, jnp.minimum(n + 1, nb - 1), 0)
    vt = lambda f: pl.BlockSpec((VAT_ROWS, BLK), lambda bi, n: (0, bi * nb + f(bi, n)[1]))
    return pl.pallas_call(
        _window_kernel,
        out_shape=jax.ShapeDtypeStruct((b, s, 512), BF16),
        grid=(b, nb),
        in_specs=[pl.BlockSpec((1, BLK, 512), cur), kv(prev), kv(cur), kv(nxt), vt(prev), vt(cur), vt(nxt),
                  pl.BlockSpec((HA, 3 * BLK, BLK), lambda bi, n: (0, 0, 0)),
                  pl.BlockSpec((HA, 128), lambda bi, n: (0, 0))],
        out_specs=pl.BlockSpec((1, BLK, 512), cur),
        compiler_params=_cparams(("parallel", "arbitrary")),
        name="window_attn",
    )(qa, ka, ka, ka, vat, vat, vat, bias_a, sink_b)


ONES_ROWS = BF16_TILE


def _diff_kernel(q_ref, k_ref, vt_ref, bias_ref, lamv_ref, g_ref, o_ref, m_scr, acc_scr, *, lam_init):
    j = pl.program_id(2)
    nk = pl.num_programs(2)
    tk = k_ref.shape[1]

    @pl.when(j == 0)
    def _():
        m_scr[...] = jnp.full(m_scr.shape, -jnp.inf, F32)
        acc_scr[...] = jnp.zeros(acc_scr.shape, F32)

    lane = lax.broadcasted_iota(jnp.int32, (1, 128), 1)
    lo = lane < DH
    zero = jnp.zeros((), BF16)
    ones = jnp.ones((ONES_ROWS, tk), BF16)

    def scores(idx):
        h, c = divmod(idx, 2)
        qh = q_ref[0, :, 128 * h:128 * (h + 1)]
        kh = k_ref[0, :, 128 * h:128 * (h + 1)]
        qm = jnp.where(lo if c == 0 else jnp.logical_not(lo), qh, zero)
        return lax.dot_general(kh, qm, NT_DIMS, preferred_element_type=F32) + bias_ref[0, h]

    def values(idx, p, alpha):
        h = idx // 2
        vaug = jnp.concatenate([vt_ref[128 * h:128 * (h + 1), :], ones], axis=0)
        acc_scr[idx] = acc_scr[idx] * alpha + jnp.dot(vaug, p, preferred_element_type=F32)

    s_next = scores(0)
    for idx in range(2 * HB):
        s = s_next
        if idx + 1 < 2 * HB:
            s_next = scores(idx + 1)
        m_old = m_scr[idx]
        m_new = jnp.maximum(m_old, jnp.max(s, axis=0, keepdims=True))
        p = jnp.exp2(s - m_new).astype(BF16)
        values(idx, p, jnp.exp2(m_old - m_new))
        m_scr[idx] = m_new

    @pl.when(j == nk - 1)
    def _():
        lv = lamv_ref[...]
        dot1 = jnp.sum(lv[0:1] * lv[1:2], axis=-1, keepdims=True)
        dot2 = jnp.sum(lv[2:3] * lv[3:4], axis=-1, keepdims=True)
        lam = jnp.exp(dot1) - jnp.exp(dot2) + lam_init
        for h in range(HB):
            a0 = acc_scr[2 * h]
            a1 = acc_scr[2 * h + 1]
            o = a0[:DVB] / a0[DVB:DVB + 1] - lam * (a1[:DVB] / a1[DVB:DVB + 1])
            o = o * lax.rsqrt(jnp.mean(o * o, axis=0, keepdims=True) + LN_EPS)
            o = jnp.transpose(o, (1, 0)) * g_ref[...] * (1.0 - lam_init)
            o_ref[0, :, 128 * h:128 * (h + 1)] = o.astype(BF16)


def _diff_attn(qb, kb, vbt, bias5, lamv, norm_g, lam_init, t):
    b, s, _ = qb.shape
    nt = s // t
    return pl.pallas_call(
        functools.partial(_diff_kernel, lam_init=lam_init),
        out_shape=jax.ShapeDtypeStruct((b, s, 512), BF16),
        grid=(b, nt, nt),
        in_specs=[pl.BlockSpec((1, t, 512), lambda bi, i, j: (bi, i, 0)),
                  pl.BlockSpec((1, t, 512), lambda bi, i, j: (bi, j, 0)),
                  pl.BlockSpec((HB * DVB, t), lambda bi, i, j: (0, bi * nt + j)),
                  pl.BlockSpec((1, HB, t, t), lambda bi, i, j: (jnp.clip(j - i, -2, 2) + 2, 0, 0, 0)),
                  pl.BlockSpec((4, DH), lambda bi, i, j: (0, 0)),
                  pl.BlockSpec((1, DVB), lambda bi, i, j: (0, 0))],
        out_specs=pl.BlockSpec((1, t, 512), lambda bi, i, j: (bi, i, 0)),
        scratch_shapes=[pltpu.VMEM((2 * HB, 1, t), F32), pltpu.VMEM((2 * HB, DVB + ONES_ROWS, t), F32)],
        compiler_params=_cparams(("parallel", "parallel", "arbitrary")),
        name="diff_attn",
    )(qb, kb, vbt, bias5, lamv, norm_g)


def _mix_kernel(a_ref, b_ref, ga_ref, gb_ref, x_ref, wa_ref, wb_ref, wo_ref, g_ref, bb_ref, o_ref):
    a = jnp.dot(a_ref[...], wa_ref[...], preferred_element_type=F32)
    b = jnp.dot(b_ref[...], wb_ref[...], preferred_element_type=F32)
    mix_in = ga_ref[...].astype(F32) * a + gb_ref[...].astype(F32) * b
    mix = jnp.dot(mix_in.astype(BF16), wo_ref[...], preferred_element_type=F32)
    o_ref[...] = _layer_norm(ALPHA * x_ref[...] + mix, g_ref[...], bb_ref[...])


def _attn_mix(a, b, ga, gb, x, wa, wb, wo, g, bb, tm=512):
    t, d = x.shape
    tile = lambda n: pl.BlockSpec((tm, n), lambda i: (i, 0))
    full = lambda r, c: pl.BlockSpec((r, c), lambda i: (0, 0))
    return pl.pallas_call(
        _mix_kernel,
        out_shape=jax.ShapeDtypeStruct((t, d), F32),
        grid=(t // tm,),
        in_specs=[tile(512), tile(512), tile(d), tile(d), tile(d),
                  full(512, d), full(512, d), full(d, d), full(1, d), full(1, d)],
        out_specs=tile(d),
        compiler_params=_cparams(("parallel",)),
        name="attn_mix",
    )(a, b, ga, gb, x, wa, wb, wo, g.reshape(1, d), bb.reshape(1, d))


N_CAND = 80
HEADS_PER_ITER = 2
BIG_IDX = 1e9


def _cand_tables():
    flat = np.full((N_CAND,), BIG_IDX, np.float32)
    valid = np.zeros((N_CAND,), np.float32)
    for i in range(16):
        flat[i] = i * 16
        valid[i] = 1
    for j in range(1, 8):
        for i in range(8):
            if (i + 1) * (j + 1) <= 16:
                r = 16 + (j - 1) * 8 + i
                flat[r] = i * 16 + j
                valid[r] = 1
    for j in range(8, 16):
        flat[72 + j - 8] = j
        valid[72 + j - 8] = 1
    return flat, valid


def _lex_argmax(v, idx):
    mx = jnp.max(v, axis=0, keepdims=True)
    mi = jnp.min(jnp.where(v == mx, idx, BIG_IDX), axis=0, keepdims=True)
    return mx, mi


def _select_kernel(x_ref, wq_ref, sk_ref, cflat_ref, cvalid_ref,
                   p1_ref, n_ref, p2_ref, r2_ref,
                   q_scr, sc_scr, work_scr, rank_scr, sort_scr, cand_scr, sel_scr):
    tt = x_ref.shape[0]
    xb = x_ref[...].astype(BF16)
    q_scr[...] = lax.dot_general(wq_ref[...], xb, NT_DIMS, preferred_element_type=F32).astype(BF16)
    key_iota = lax.broadcasted_iota(jnp.int32, (N_KEYS, tt), 0).astype(F32)
    neg_inf = jnp.float32(-jnp.inf)

    def extract(exact):
        work_scr[...] = sc_scr[...]
        rank_scr[...] = jnp.full(rank_scr.shape, float(PEER_TOPK), F32)

        def round_body(r, c):
            for ch in range(2 * HEADS_PER_ITER):
                w = work_scr[ch]
                if exact:
                    mx, mi = _lex_argmax(w, key_iota)
                    hit = key_iota == mi
                else:
                    mx = jnp.max(w, axis=0, keepdims=True)
                    hit = w == mx
                work_scr[ch] = jnp.where(hit, neg_inf, w)
                rank_scr[ch] = jnp.where(hit, jnp.asarray(r, F32), rank_scr[ch])
                sort_scr[ch, pl.ds(r, 1), :] = mx
            return c

        lax.fori_loop(0, PEER_TOPK, round_body, 0)
        taken = jnp.where(rank_scr[...] < float(PEER_TOPK), 1.0, 0.0)
        return [jnp.sum(taken[ch], axis=0, keepdims=True) for ch in range(2 * HEADS_PER_ITER)]

    def pair_stage(exact):
        cflat = cflat_ref[...]
        tops = []
        for g in range(HEADS_PER_ITER):
            s1 = sort_scr[2 * g]
            s2 = sort_scr[2 * g + 1]
            tops.append(s1[0:1] + s2[0:1])
            groups = [s1 + s2[0:1]]
            for j in range(1, 8):
                groups.append(s1[0:8] + s2[j:j + 1])
            groups.append(s1[0:1] + s2[8:16])
            cand_scr[g] = jnp.where(cvalid_ref[...] > 0.5, jnp.concatenate(groups, axis=0), neg_inf)
        sel_scr[...] = jnp.zeros(sel_scr.shape, F32)

        def cand_round(r, zs):
            out = []
            for g in range(HEADS_PER_ITER):
                w = cand_scr[g]
                if exact:
                    mx, mi = _lex_argmax(w, cflat)
                    hit = cflat == mi
                else:
                    mx = jnp.max(w, axis=0, keepdims=True)
                    hit = w == mx
                cand_scr[g] = jnp.where(hit, neg_inf, w)
                sel_scr[g] = jnp.where(hit, 1.0, sel_scr[g])
                out.append(zs[g] + jnp.exp(mx - tops[g]))
            return tuple(out)

        zs = lax.fori_loop(0, PEER_TOPK, cand_round,
                           tuple(jnp.zeros((1, tt), F32) for _ in range(HEADS_PER_ITER)))
        return zs, [jnp.sum(sel_scr[g], axis=0, keepdims=True) for g in range(HEADS_PER_ITER)]

    def finish(h, g, z):
        s1 = sort_scr[2 * g]
        s2 = sort_scr[2 * g + 1]
        sel = sel_scr[g]
        cnt_lo = sel[0:8]
        for k in range(1, 8):
            cnt_lo = cnt_lo + sel[8 + 8 * k:16 + 8 * k]
        extra = jnp.sum(sel[72:80], axis=0, keepdims=True)
        row0 = lax.broadcasted_iota(jnp.int32, (8, tt), 0) == 0
        cnt_lo = cnt_lo + jnp.where(row0, extra, 0.0)
        cnt = jnp.concatenate([cnt_lo, sel[8:16]], axis=0)

        r1 = rank_scr[2 * g]
        r2 = rank_scr[2 * g + 1]
        nsel = jnp.zeros((N_KEYS, tt), F32)
        for i in range(PEER_TOPK):
            nsel = jnp.where(r1 == float(i), cnt[i:i + 1], nsel)
        in1 = r1 < float(PEER_TOPK)
        in2 = r2 < float(PEER_TOPK)
        p1 = jnp.where(in1, jnp.exp(sc_scr[2 * g] - s1[0:1]), 0.0) * (0.5 / z)
        p2 = jnp.where(in2, jnp.exp(sc_scr[2 * g + 1] - s2[0:1]), 0.0)
        p1_ref[h] = p1
        n_ref[h] = nsel
        p2_ref[h] = p2.astype(BF16).reshape(N_KEYS // BF16_TILE, BF16_TILE, tt)
        r2_ref[h] = r2.astype(BF16).reshape(N_KEYS // BF16_TILE, BF16_TILE, tt)

    def iter_body(it, carry):
        h0 = it * HEADS_PER_ITER
        for ch in range(2 * HEADS_PER_ITER):
            hp = 2 * h0 + ch
            off = pl.multiple_of(hp * 128, 128)
            sc_scr[ch] = jnp.dot(sk_ref[hp], q_scr[pl.ds(off, 128), :], preferred_element_type=F32)
        off16 = [jnp.abs(t - float(PEER_TOPK)) for t in extract(False)]
        zs, marks = pair_stage(False)
        off16 += [jnp.abs(m - float(PEER_TOPK)) for m in marks]
        for g in range(HEADS_PER_ITER):
            finish(h0 + g, g, zs[g])

        @pl.when(jnp.max(sum(off16)) > 0.0)
        def _():
            extract(True)
            zs_exact = pair_stage(True)[0]
            for g in range(HEADS_PER_ITER):
                finish(h0 + g, g, zs_exact[g])

        return carry

    lax.fori_loop(0, PEER_HEADS // HEADS_PER_ITER, iter_body, 0)


def _peer_select(x, wq_t, sk, tt=256):
    t, d = x.shape
    cflat, cvalid = _cand_tables()
    cflat = jnp.asarray(np.broadcast_to(cflat[:, None], (N_CAND, tt)))
    cvalid = jnp.asarray(np.broadcast_to(cvalid[:, None], (N_CAND, tt)))
    out = jax.ShapeDtypeStruct((PEER_HEADS, N_KEYS, t), F32)
    out16 = jax.ShapeDtypeStruct((PEER_HEADS, N_KEYS // BF16_TILE, BF16_TILE, t), BF16)
    ospec = pl.BlockSpec((PEER_HEADS, N_KEYS, tt), lambda i: (0, 0, i))
    ospec16 = pl.BlockSpec((PEER_HEADS, N_KEYS // BF16_TILE, BF16_TILE, tt), lambda i: (0, 0, 0, i))
    return pl.pallas_call(
        _select_kernel,
        out_shape=[out, out, out16, out16],
        grid=(t // tt,),
        in_specs=[pl.BlockSpec((tt, d), lambda i: (i, 0)),
                  pl.BlockSpec((2 * PEER_HEADS * 128, d), lambda i: (0, 0)),
                  pl.BlockSpec((2 * PEER_HEADS, N_KEYS, 128), lambda i: (0, 0, 0)),
                  pl.BlockSpec((N_CAND, tt), lambda i: (0, 0)),
                  pl.BlockSpec((N_CAND, tt), lambda i: (0, 0))],
        out_specs=[ospec, ospec, ospec16, ospec16],
        scratch_shapes=[pltpu.VMEM((2 * PEER_HEADS * 128, tt), BF16),
                        pltpu.VMEM((2 * HEADS_PER_ITER, N_KEYS, tt), F32),
                        pltpu.VMEM((2 * HEADS_PER_ITER, N_KEYS, tt), F32),
                        pltpu.VMEM((2 * HEADS_PER_ITER, N_KEYS, tt), F32),
                        pltpu.VMEM((2 * HEADS_PER_ITER, PEER_TOPK, tt), F32),
                        pltpu.VMEM((HEADS_PER_ITER, N_CAND, tt), F32),
                        pltpu.VMEM((HEADS_PER_ITER, N_CAND, tt), F32)],
        compiler_params=_cparams(("parallel",)),
        name="peer_select",
    )(x, wq_t, sk, cflat, cvalid)


SUB_E = 256


def _peer_kernel(x_ref, u_ref, vt_ref, p1_ref, n_ref, p2_ref, r2_ref, o_ref, xb_scr, w_scr, acc_scr):
    e = pl.program_id(1)
    ne = pl.num_programs(1)
    eb = u_ref.shape[0]
    tt = x_ref.shape[0]

    @pl.when(e == 0)
    def _():
        xb_scr[...] = x_ref[...].astype(BF16)
        acc_scr[...] = jnp.zeros(acc_scr.shape, F32)

    xb = xb_scr[...]
    inv_sqrt2 = 1.0 / math.sqrt(2.0)
    for sb in range(eb // SUB_E):
        hsub = lax.dot_general(u_ref[sb * SUB_E:(sb + 1) * SUB_E, :], xb, NT_DIMS,
                               preferred_element_type=F32)
        for ai in range(SUB_E // N_KEYS):
            a = sb * (SUB_E // N_KEYS) + ai
            hh = hsub[ai * N_KEYS:(ai + 1) * N_KEYS]
            gate = None
            for h in range(PEER_HEADS):
                nrow = jnp.broadcast_to(n_ref[h, a:a + 1, :], (BF16_TILE, tt)).astype(BF16)[None]
                prow = jnp.broadcast_to(p1_ref[h, a:a + 1, :], (BF16_TILE, tt)).astype(BF16)[None]
                term = jnp.where(r2_ref[h] < nrow, p2_ref[h], jnp.zeros((), BF16)) * prow
                gate = term if gate is None else gate + term
            act = hh * (1.0 + lax.erf(hh * inv_sqrt2))
            wgt = act.astype(BF16).reshape(N_KEYS // BF16_TILE, BF16_TILE, tt) * gate
            w_scr[a * N_KEYS:(a + 1) * N_KEYS, :] = wgt.reshape(N_KEYS, tt)
    acc_scr[...] += jnp.dot(vt_ref[...], w_scr[...], preferred_element_type=F32)

    @pl.when(e == ne - 1)
    def _():
        o_ref[...] = acc_scr[...]


def _peer_dense(x, u, vt, p1, nsel, p2, r2, tt=512, eb=2048):
    t, d = x.shape
    ra = eb // N_KEYS
    rows = pl.BlockSpec((PEER_HEADS, ra, tt), lambda i, e: (0, e, i))
    full = pl.BlockSpec((PEER_HEADS, N_KEYS // BF16_TILE, BF16_TILE, tt), lambda i, e: (0, 0, 0, i))
    return pl.pallas_call(
        _peer_kernel,
        out_shape=jax.ShapeDtypeStruct((d, t), F32),
        grid=(t // tt, N_EXPERTS // eb),
        in_specs=[pl.BlockSpec((tt, d), lambda i, e: (i, 0)),
                  pl.BlockSpec((eb, d), lambda i, e: (e, 0)),
                  pl.BlockSpec((d, eb), lambda i, e: (0, e)),
                  rows, rows, full, full],
        out_specs=pl.BlockSpec((d, tt), lambda i, e: (0, i)),
        scratch_shapes=[pltpu.VMEM((tt, d), BF16), pltpu.VMEM((eb, tt), BF16), pltpu.VMEM((d, tt), F32)],
        compiler_params=_cparams(("parallel", "arbitrary")),
        name="peer_dense",
    )(x, u, vt, p1, nsel, p2, r2)


def _final_kernel(x_ref, cht_ref, p_ref, wg_ref, wp_ref, g_ref, b_ref, o_ref):
    x = x_ref[...]
    gate = jax.nn.sigmoid(jnp.dot(x.astype(BF16), wg_ref[...], preferred_element_type=F32))
    ple = gate * jnp.dot(p_ref[...].astype(BF16), wp_ref[...], preferred_element_type=F32)
    ch = jnp.transpose(cht_ref[...], (1, 0))
    o_ref[...] = _layer_norm(ALPHA * x + ch + ple, g_ref[...], b_ref[...])


def _final(x, cht, p, wg, wp, g, b, tm=512):
    t, d = x.shape
    full = lambda r, c: pl.BlockSpec((r, c), lambda i: (0, 0))
    return pl.pallas_call(
        _final_kernel,
        out_shape=jax.ShapeDtypeStruct((t, d), F32),
        grid=(t // tm,),
        in_specs=[pl.BlockSpec((tm, d), lambda i: (i, 0)),
                  pl.BlockSpec((d, tm), lambda i: (0, i)),
                  pl.BlockSpec((tm, PLE_DIM), lambda i: (i, 0)),
                  full(d, d), full(PLE_DIM, d), full(1, d), full(1, d)],
        out_specs=pl.BlockSpec((tm, d), lambda i: (i, 0)),
        compiler_params=_cparams(("parallel",)),
        name="ple_ln2",
    )(x, cht, p, wg, wp, g.reshape(1, d), b.reshape(1, d))


def _diff_tile(s):
    return min(512, s)


def _trunk(x, p, prm, layers):
    b, s, d = x.shape
    t = b * s
    x = _embed_ln(x.reshape(t, d), prm["emb_g"], prm["emb_b"])
    for i, lw in enumerate(layers):
        qa, ka, vat, qb, kb, vbt, ga, gb = _in_proj(x, lw["w_in"], lw["w_vbt"])
        r3 = lambda m: m.reshape(b, s, m.shape[-1])
        a = _window_attn(r3(qa), r3(ka), vat, lw["bias_a"], lw["sink"])
        td = _diff_tile(s)
        bd = _diff_attn(r3(qb), r3(kb), vbt, lw["bias_b"][td], lw["lamv"], lw["norm_g"], lw["lam_init"], td)
        x = _attn_mix(a.reshape(t, 512), bd.reshape(t, 512), ga, gb, x,
                      lw["w_a"], lw["w_b"], lw["w_o"], lw["ln1_g"], lw["ln1_b"])
        p1, nsel, p2, r2 = _peer_select(x, lw["wq_t"], lw["subkeys"])
        cht = _peer_dense(x, lw["u"], lw["v_t"], p1, nsel, p2, r2)
        x = _final(x, cht, p[i].reshape(t, PLE_DIM), lw["ple_gate_w"], lw["ple_w"], lw["ln2_g"], lw["ln2_b"])
    return x.reshape(b, s, d)


def kernel(x_prompt, x_sample, p_prompt, p_sample, emb_ln_g, emb_ln_b, rel_bias, w_in, w_a, w_b, w_o, sink,
           lam_q1, lam_k1, lam_q2, lam_k2, diff_norm_g, ln1_g, ln1_b, peer_wq, peer_subkeys, peer_u, peer_v,
           ln2_g, ln2_b, ple_w, ple_gate_w):
    table_a = rel_bias[:, :HA]
    table_b = rel_bias[:, HA:]
    bias_a = _window_bias(table_a)
    diff_tiles = sorted({_diff_tile(x_prompt.shape[1]), _diff_tile(x_sample.shape[1])})
    bias_b = {t: _diff_bias(table_b, t) for t in diff_tiles}
    layers = []
    for i in range(DEPTH):
        layers.append(dict(
            w_in=_prep_w_in(w_in[i])[0],
            w_vbt=_prep_w_in(w_in[i])[1],
            bias_a=bias_a,
            bias_b=bias_b,
            sink=jnp.broadcast_to(sink[i].astype(F32)[:, None], (HA, 128)),
            lamv=jnp.stack([lam_q1[i], lam_k1[i], lam_q2[i], lam_k2[i]], axis=0).astype(F32),
            norm_g=diff_norm_g[i].astype(F32).reshape(1, DVB),
            lam_init=0.8 - 0.6 * math.exp(-0.3 * i),
            w_a=w_a[i].astype(BF16), w_b=w_b[i].astype(BF16), w_o=w_o[i].astype(BF16),
            ln1_g=ln1_g[i], ln1_b=ln1_b[i],
            wq_t=jnp.transpose(peer_wq[i]).astype(BF16),
            subkeys=peer_subkeys[i].reshape(2 * PEER_HEADS, N_KEYS, 128).astype(BF16),
            u=peer_u[i].astype(BF16),
            v_t=jnp.transpose(peer_v[i]).astype(BF16),
            ple_gate_w=ple_gate_w[i].astype(BF16), ple_w=ple_w[i].astype(BF16),
            ln2_g=ln2_g[i], ln2_b=ln2_b[i],
        ))
    prm = dict(emb_g=emb_ln_g, emb_b=emb_ln_b)
    y_prompt = _trunk(x_prompt, p_prompt, prm, layers)
    y_sample = _trunk(x_sample, p_sample, prm, layers)
    return (y_prompt, y_sample)
```

```python
import functools
import math

import numpy as np
import jax
import jax.numpy as jnp
from jax import lax
from jax.experimental import pallas as pl
from jax.experimental.pallas import tpu as pltpu

F32 = jnp.float32
BF16 = jnp.bfloat16

D_MODEL = 1024
DEPTH = 2
HA, HKV, DH = 8, 2, 64
WINDOW = 128
BLK = 128
HB = 4
DVB = 2 * DH
N_BUCKETS = 32
MAX_DIST = 128
N_KEYS = 128
N_EXPERTS = N_KEYS * N_KEYS
PEER_HEADS = 8
PEER_TOPK = 16
PLE_DIM = 256
ALPHA = (2.0 * DEPTH) ** 0.25
LN_EPS = 1e-5
NEG = -1e30
LOG2E = math.log2(math.e)

VMEM_LIMIT = 56 * 1024 * 1024
BF16_TILE = 16

NT_DIMS = (((1,), (1,)), ((), ()))


def _cparams(sem):
    return pltpu.CompilerParams(dimension_semantics=sem, vmem_limit_bytes=VMEM_LIMIT)


def _layer_norm(x, g, b):
    mu = jnp.mean(x, axis=-1, keepdims=True)
    xc = x - mu
    var = jnp.mean(xc * xc, axis=-1, keepdims=True)
    return xc * lax.rsqrt(var + LN_EPS) * g + b


_QA0, _KA0, _QB0, _KB0, _GA0, _GB0, _INW = 0, 512, 768, 1280, 1792, 2816, 3840
VAT_ROWS = HKV * DH


def _project(xf, w_ref, wvt_ref, qa_ref, ka_ref, vat_ref, qb_ref, kb_ref, vbt_ref, ga_ref, gb_ref):
    x = xf.astype(BF16)

    def proj(lo, hi):
        return jnp.dot(x, w_ref[:, lo:hi], preferred_element_type=F32)

    qa_ref[...] = proj(_QA0, _KA0).astype(BF16)
    ka_ref[...] = proj(_KA0, _QB0).astype(BF16)
    qb_ref[...] = proj(_QB0, _KB0).astype(BF16)
    kb_ref[...] = proj(_KB0, _GA0).astype(BF16)
    vt = lax.dot_general(wvt_ref[...], x, NT_DIMS, preferred_element_type=F32).astype(BF16)
    vat_ref[...] = vt[:VAT_ROWS]
    vbt_ref[...] = vt[VAT_ROWS:]
    ga_ref[...] = jax.nn.sigmoid(proj(_GA0, _GB0)).astype(BF16)
    gb_ref[...] = jax.nn.sigmoid(proj(_GB0, _INW)).astype(BF16)


def _inproj_kernel(x_ref, w_ref, wvt_ref, *out_refs):
    _project(x_ref[...], w_ref, wvt_ref, *out_refs)


def _ln_inproj_kernel(x_ref, g_ref, b_ref, w_ref, wvt_ref, xln_ref, *out_refs):
    xln = _layer_norm(x_ref[...], g_ref[...], b_ref[...])
    xln_ref[...] = xln
    _project(xln, w_ref, wvt_ref, *out_refs)


def _in_proj(x, w, wvt, ln=None, tm=512):
    t, d = x.shape
    rows = lambda n: (jax.ShapeDtypeStruct((t, n), BF16), pl.BlockSpec((tm, n), lambda i: (i, 0)))
    cols = lambda n: (jax.ShapeDtypeStruct((n, t), BF16), pl.BlockSpec((n, tm), lambda i: (0, i)))
    outs = [rows(512), rows(256), cols(VAT_ROWS), rows(512), rows(512), cols(HB * DVB), rows(d), rows(d)]
    xspec = pl.BlockSpec((tm, d), lambda i: (i, 0))
    wspecs = [pl.BlockSpec((d, _INW), lambda i: (0, 0)), pl.BlockSpec((VAT_ROWS + HB * DVB, d), lambda i: (0, 0))]
    if ln is None:
        body, ins, in_specs = _inproj_kernel, (x, w, wvt), [xspec] + wspecs
    else:
        vec = pl.BlockSpec((1, d), lambda i: (0, 0))
        body, ins = _ln_inproj_kernel, (x, ln[0].reshape(1, d), ln[1].reshape(1, d), w, wvt)
        in_specs = [xspec, vec, vec] + wspecs
        outs = [(jax.ShapeDtypeStruct((t, d), F32), xspec)] + outs
    return pl.pallas_call(
        body,
        out_shape=[o[0] for o in outs],
        grid=(t // tm,),
        in_specs=in_specs,
        out_specs=[o[1] for o in outs],
        compiler_params=_cparams(("parallel",)),
        name="in_proj",
    )(*ins)


def _prep_w_in(w):
    sc = DH ** -0.5
    qa = w[:, 0:512] * sc
    ka = w[:, 512:640]
    va = w[:, 640:768]
    qb = w[:, 768:1280] * (sc * LOG2E)
    kb = w[:, 1280:1792]
    vb = w[:, 1792:2304]
    gates = w[:, 2304:]
    dup = lambda m: jnp.concatenate([m[:, 0:64], m[:, 0:64], m[:, 64:128], m[:, 64:128]], axis=1)
    main = jnp.concatenate([qa, dup(ka), qb, kb, gates], axis=1).astype(BF16)
    return main, jnp.transpose(jnp.concatenate([va, vb], axis=1)).astype(BF16)


def _t5_bucket(rel):
    half = N_BUCKETS // 2
    max_exact = half // 2
    ret = jnp.where(rel > 0, half, 0)
    n = jnp.abs(rel)
    nf = jnp.maximum(n, 1).astype(F32)
    large = max_exact + (jnp.log(nf / max_exact) / math.log(MAX_DIST / max_exact)
                         * (half - max_exact)).astype(jnp.int32)
    large = jnp.minimum(large, half - 1)
    return ret + jnp.where(n < max_exact, n, large)


def _bias_lookup(table, bucket):
    tab = table.astype(F32)
    out = jnp.zeros((tab.shape[1],) + bucket.shape, F32)
    for bk in range(N_BUCKETS):
        out = jnp.where((bucket == bk)[None], tab[bk].reshape((-1,) + (1,) * bucket.ndim), out)
    return out


def _window_bias(table_a):
    kpos = jnp.arange(3 * BLK, dtype=jnp.int32)[:, None]
    qpos = jnp.arange(BLK, dtype=jnp.int32)[None, :]
    rel = kpos - BLK - qpos
    bias = _bias_lookup(table_a, _t5_bucket(rel))
    return jnp.where((jnp.abs(rel) <= WINDOW)[None], bias, NEG)


def _diff_bias(table_b, t):
    kpos = jnp.arange(t, dtype=jnp.int32)[:, None]
    qpos = jnp.arange(t, dtype=jnp.int32)[None, :]
    tiles = []
    for d in (-2, -1, 0, 1, 2):
        rel = d * t + kpos - qpos
        tiles.append(_bias_lookup(table_b, _t5_bucket(rel)) * LOG2E)
    return jnp.stack(tiles, axis=0)


def _window_kernel(q_ref, kp_ref, kc_ref, kn_ref, vp_ref, vc_ref, vn_ref, bias_ref, sink_ref, o_ref):
    n = pl.program_id(1)
    nb = pl.num_programs(1)
    q = q_ref[0]
    k3 = jnp.concatenate([kp_ref[0], kc_ref[0], kn_ref[0]], axis=0)
    v3t = jnp.concatenate([vp_ref[...], vc_ref[...], vn_ref[...]], axis=1)
    lane = lax.broadcasted_iota(jnp.int32, (1, 128), 1)
    lo = lane < DH
    row = lax.broadcasted_iota(jnp.int32, (3 * BLK, BLK), 0)
    off_edge = ((row < BLK) & (n == 0)) | ((row >= 2 * BLK) & (n == nb - 1))
    edge = jnp.where(off_edge, NEG, 0.0).astype(F32)
    zero = jnp.zeros((), BF16)
    kv_of = lambda hd: hd // (HA // HKV)
    scores = []
    for hd in range(HA):
        pair, half = divmod(hd, 2)
        q2 = q[:, 128 * pair:128 * (pair + 1)]
        qm = jnp.where(lo if half == 0 else jnp.logical_not(lo), q2, zero)
        kk = k3[:, 128 * kv_of(hd):128 * (kv_of(hd) + 1)]
        s = lax.dot_general(kk, qm, NT_DIMS, preferred_element_type=F32)
        scores.append(s + bias_ref[hd] + edge)
    probs, denoms = [], []
    for hd in range(HA):
        s = scores[hd]
        sk = sink_ref[hd:hd + 1, 0:1]
        m = jnp.maximum(jnp.max(s, axis=0, keepdims=True), sk)
        p = jnp.exp(s - m)
        denoms.append(jnp.sum(p, axis=0, keepdims=True) + jnp.exp(sk - m))
        probs.append(p.astype(BF16))
    outs = []
    for hd in range(HA):
        vv = v3t[DH * kv_of(hd):DH * (kv_of(hd) + 1), :]
        o = jnp.dot(vv, probs[hd], preferred_element_type=F32)
        outs.append(o / denoms[hd])
    for pair in range(HA // 2):
        o2 = jnp.concatenate([outs[2 * pair], outs[2 * pair + 1]], axis=0)
        o_ref[0, :, 128 * pair:128 * (pair + 1)] = jnp.transpose(o2, (1, 0)).astype(BF16)


def _window_attn(qa, ka, vat, bias_a, sink_b):
    b, s, _ = qa.shape
    nb = s // BLK
    kv = lambda f: pl.BlockSpec((1, BLK, 256), f)
    prev = lambda bi, n: (bi, jnp.maximum(n - 1, 0), 0)
    cur = lambda bi, n: (bi, n, 0)
    nxt = lambda bi, n: (bi, jnp.minimum(n + 1, nb - 1), 0)
    vt = lambda f: pl.BlockSpec((VAT_ROWS, BLK), lambda bi, n: (0, bi * nb + f(bi, n)[1]))
    return pl.pallas_call(
        _window_kernel,
        out_shape=jax.ShapeDtypeStruct((b, s, 512), BF16),
        grid=(b, nb),
        in_specs=[pl.BlockSpec((1, BLK, 512), cur), kv(prev), kv(cur), kv(nxt), vt(prev), vt(cur), vt(nxt),
                  pl.BlockSpec((HA, 3 * BLK, BLK), lambda bi, n: (0, 0, 0)),
                  pl.BlockSpec((HA, 128), lambda bi, n: (0, 0))],
        out_specs=pl.BlockSpec((1, BLK, 512), cur),
        compiler_params=_cparams(("parallel", "arbitrary")),
        name="window_attn",
    )(qa, ka, ka, ka, vat, vat, vat, bias_a, sink_b)


ONES_ROWS = BF16_TILE


def _diff_kernel(q_ref, k_ref, vt_ref, bias_ref, lamv_ref, g_ref, o_ref, m_scr, acc_scr, *, lam_init):
    j = pl.program_id(2)
    nk = pl.num_programs(2)
    tk = k_ref.shape[1]

    @pl.when(j == 0)
    def _():
        m_scr[...] = jnp.full(m_scr.shape, -jnp.inf, F32)
        acc_scr[...] = jnp.zeros(acc_scr.shape, F32)

    lane = lax.broadcasted_iota(jnp.int32, (1, 128), 1)
    lo = lane < DH
    zero = jnp.zeros((), BF16)
    ones = jnp.ones((ONES_ROWS, tk), BF16)

    def scores(idx):
        h, c = divmod(idx, 2)
        qh = q_ref[0, :, 128 * h:128 * (h + 1)]
        kh = k_ref[0, :, 128 * h:128 * (h + 1)]
        qm = jnp.where(lo if c == 0 else jnp.logical_not(lo), qh, zero)
        return lax.dot_general(kh, qm, NT_DIMS, preferred_element_type=F32) + bias_ref[0, h]

    def values(idx, p, alpha):
        h = idx // 2
        vaug = jnp.concatenate([vt_ref[128 * h:128 * (h + 1), :], ones], axis=0)
        acc_scr[idx] = acc_scr[idx] * alpha + jnp.dot(vaug, p, preferred_element_type=F32)

    s_next = scores(0)
    for idx in range(2 * HB):
        s = s_next
        if idx + 1 < 2 * HB:
            s_next = scores(idx + 1)
        m_old = m_scr[idx]
        m_new = jnp.maximum(m_old, jnp.max(s, axis=0, keepdims=True))
        p = jnp.exp2(s - m_new).astype(BF16)
        values(idx, p, jnp.exp2(m_old - m_new))
        m_scr[idx] = m_new

    @pl.when(j == nk - 1)
    def _():
        lv = lamv_ref[...]
        dot1 = jnp.sum(lv[0:1] * lv[1:2], axis=-1, keepdims=True)
        dot2 = jnp.sum(lv[2:3] * lv[3:4], axis=-1, keepdims=True)
        lam = jnp.exp(dot1) - jnp.exp(dot2) + lam_init
        for h in range(HB):
            a0 = acc_scr[2 * h]
            a1 = acc_scr[2 * h + 1]
            o = a0[:DVB] / a0[DVB:DVB + 1] - lam * (a1[:DVB] / a1[DVB:DVB + 1])
            o = o * lax.rsqrt(jnp.mean(o * o, axis=0, keepdims=True) + LN_EPS)
            o = jnp.transpose(o, (1, 0)) * g_ref[...] * (1.0 - lam_init)
            o_ref[0, :, 128 * h:128 * (h + 1)] = o.astype(BF16)


def _diff_attn(qb, kb, vbt, bias5, lamv, norm_g, lam_init, t):
    b, s, _ = qb.shape
    nt = s // t
    return pl.pallas_call(
        functools.partial(_diff_kernel, lam_init=lam_init),
        out_shape=jax.ShapeDtypeStruct((b, s, 512), BF16),
        grid=(b, nt, nt),
        in_specs=[pl.BlockSpec((1, t, 512), lambda bi, i, j: (bi, i, 0)),
                  pl.BlockSpec((1, t, 512), lambda bi, i, j: (bi, j, 0)),
                  pl.BlockSpec((HB * DVB, t), lambda bi, i, j: (0, bi * nt + j)),
                  pl.BlockSpec((1, HB, t, t), lambda bi, i, j: (jnp.clip(j - i, -2, 2) + 2, 0, 0, 0)),
                  pl.BlockSpec((4, DH), lambda bi, i, j: (0, 0)),
                  pl.BlockSpec((1, DVB), lambda bi, i, j: (0, 0))],
        out_specs=pl.BlockSpec((1, t, 512), lambda bi, i, j: (bi, i, 0)),
        scratch_shapes=[pltpu.VMEM((2 * HB, 1, t), F32), pltpu.VMEM((2 * HB, DVB + ONES_ROWS, t), F32)],
        compiler_params=_cparams(("parallel", "parallel", "arbitrary")),
        name="diff_attn",
    )(qb, kb, vbt, bias5, lamv, norm_g)


def _mix_kernel(a_ref, b_ref, ga_ref, gb_ref, x_ref, wa_ref, wb_ref, wo_ref, g_ref, bb_ref, o_ref):
    a = jnp.dot(a_ref[...], wa_ref[...], preferred_element_type=F32)
    b = jnp.dot(b_ref[...], wb_ref[...], preferred_element_type=F32)
    mix_in = ga_ref[...].astype(F32) * a + gb_ref[...].astype(F32) * b
    mix = jnp.dot(mix_in.astype(BF16), wo_ref[...], preferred_element_type=F32)
    o_ref[...] = _layer_norm(ALPHA * x_ref[...] + mix, g_ref[...], bb_ref[...])


def _attn_mix(a, b, ga, gb, x, wa, wb, wo, g, bb, tm=512):
    t, d = x.shape
    tile = lambda n: pl.BlockSpec((tm, n), lambda i: (i, 0))
    full = lambda r, c: pl.BlockSpec((r, c), lambda i: (0, 0))
    return pl.pallas_call(
        _mix_kernel,
        out_shape=jax.ShapeDtypeStruct((t, d), F32),
        grid=(t // tm,),
        in_specs=[tile(512), tile(512), tile(d), tile(d), tile(d),
                  full(512, d), full(512, d), full(d, d), full(1, d), full(1, d)],
        out_specs=tile(d),
        compiler_params=_cparams(("parallel",)),
        name="attn_mix",
    )(a, b, ga, gb, x, wa, wb, wo, g.reshape(1, d), bb.reshape(1, d))


N_CAND = 80
HEADS_PER_ITER = 2
BIG_IDX = 1e9


def _cand_tables():
    flat = np.full((N_CAND,), BIG_IDX, np.float32)
    valid = np.zeros((N_CAND,), np.float32)
    for i in range(16):
        flat[i] = i * 16
        valid[i] = 1
    for j in range(1, 8):
        for i in range(8):
            if (i + 1) * (j + 1) <= 16:
                r = 16 + (j - 1) * 8 + i
                flat[r] = i * 16 + j
                valid[r] = 1
    for j in range(8, 16):
        flat[72 + j - 8] = j
        valid[72 + j - 8] = 1
    return flat, valid


def _lex_argmax(v, idx):
    mx = jnp.max(v, axis=0, keepdims=True)
    mi = jnp.min(jnp.where(v == mx, idx, BIG_IDX), axis=0, keepdims=True)
    return mx, mi


def _select_kernel(x_ref, wq_ref, sk_ref, cflat_ref, cvalid_ref,
                   p1_ref, n_ref, p2_ref, r2_ref,
                   q_scr, sc_scr, work_scr, rank_scr, sort_scr, cand_scr, sel_scr):
    tt = x_ref.shape[0]
    xb = x_ref[...].astype(BF16)
    q_scr[...] = lax.dot_general(wq_ref[...], xb, NT_DIMS, preferred_element_type=F32).astype(BF16)
    key_iota = lax.broadcasted_iota(jnp.int32, (N_KEYS, tt), 0).astype(F32)
    neg_inf = jnp.float32(-jnp.inf)

    def extract(exact):
        work_scr[...] = sc_scr[...]
        rank_scr[...] = jnp.full(rank_scr.shape, float(PEER_TOPK), F32)

        def round_body(r, c):
            for ch in range(2 * HEADS_PER_ITER):
                w = work_scr[ch]
                if exact:
                    mx, mi = _lex_argmax(w, key_iota)
                    hit = key_iota == mi
                else:
                    mx = jnp.max(w, axis=0, keepdims=True)
                    hit = w == mx
                work_scr[ch] = jnp.where(hit, neg_inf, w)
                rank_scr[ch] = jnp.where(hit, jnp.asarray(r, F32), rank_scr[ch])
                sort_scr[ch, pl.ds(r, 1), :] = mx
            return c

        lax.fori_loop(0, PEER_TOPK, round_body, 0)
        taken = jnp.where(rank_scr[...] < float(PEER_TOPK), 1.0, 0.0)
        return [jnp.sum(taken[ch], axis=0, keepdims=True) for ch in range(2 * HEADS_PER_ITER)]

    def pair_stage(exact):
        cflat = cflat_ref[...]
        tops = []
        for g in range(HEADS_PER_ITER):
            s1 = sort_scr[2 * g]
            s2 = sort_scr[2 * g + 1]
            tops.append(s1[0:1] + s2[0:1])
            groups = [s1 + s2[0:1]]
            for j in range(1, 8):
                groups.append(s1[0:8] + s2[j:j + 1])
            groups.append(s1[0:1] + s2[8:16])
            cand_scr[g] = jnp.where(cvalid_ref[...] > 0.5, jnp.concatenate(groups, axis=0), neg_inf)
        sel_scr[...] = jnp.zeros(sel_scr.shape, F32)

        def cand_round(r, zs):
            out = []
            for g in range(HEADS_PER_ITER):
                w = cand_scr[g]
                if exact:
                    mx, mi = _lex_argmax(w, cflat)
                    hit = cflat == mi
                else:
                    mx = jnp.max(w, axis=0, keepdims=True)
                    hit = w == mx
                cand_scr[g] = jnp.where(hit, neg_inf, w)
                sel_scr[g] = jnp.where(hit, 1.0, sel_scr[g])
                out.append(zs[g] + jnp.exp(mx - tops[g]))
            return tuple(out)

        zs = lax.fori_loop(0, PEER_TOPK, cand_round,
                           tuple(jnp.zeros((1, tt), F32) for _ in range(HEADS_PER_ITER)))
        return zs, [jnp.sum(sel_scr[g], axis=0, keepdims=True) for g in range(HEADS_PER_ITER)]

    def finish(h, g, z):
        s1 = sort_scr[2 * g]
        s2 = sort_scr[2 * g + 1]
        sel = sel_scr[g]
        cnt_lo = sel[0:8]
        for k in range(1, 8):
            cnt_lo = cnt_lo + sel[8 + 8 * k:16 + 8 * k]
        extra = jnp.sum(sel[72:80], axis=0, keepdims=True)
        row0 = lax.broadcasted_iota(jnp.int32, (8, tt), 0) == 0
        cnt_lo = cnt_lo + jnp.where(row0, extra, 0.0)
        cnt = jnp.concatenate([cnt_lo, sel[8:16]], axis=0)

        r1 = rank_scr[2 * g]
        r2 = rank_scr[2 * g + 1]
        nsel = jnp.zeros((N_KEYS, tt), F32)
        for i in range(PEER_TOPK):
            nsel = jnp.where(r1 == float(i), cnt[i:i + 1], nsel)
        in1 = r1 < float(PEER_TOPK)
        in2 = r2 < float(PEER_TOPK)
        p1 = jnp.where(in1, jnp.exp(sc_scr[2 * g] - s1[0:1]), 0.0) * (0.5 / z)
        p2 = jnp.where(in2, jnp.exp(sc_scr[2 * g + 1] - s2[0:1]), 0.0)
        p1_ref[h] = p1
        n_ref[h] = nsel
        p2_ref[h] = p2.astype(BF16).reshape(N_KEYS // BF16_TILE, BF16_TILE, tt)
        r2_ref[h] = r2.astype(BF16).reshape(N_KEYS // BF16_TILE, BF16_TILE, tt)

    def iter_body(it, carry):
        h0 = it * HEADS_PER_ITER
        for ch in range(2 * HEADS_PER_ITER):
            hp = 2 * h0 + ch
            off = pl.multiple_of(hp * 128, 128)
            sc_scr[ch] = jnp.dot(sk_ref[hp], q_scr[pl.ds(off, 128), :], preferred_element_type=F32)
        off16 = [jnp.abs(t - float(PEER_TOPK)) for t in extract(False)]
        zs, marks = pair_stage(False)
        off16 += [jnp.abs(m - float(PEER_TOPK)) for m in marks]
        for g in range(HEADS_PER_ITER):
            finish(h0 + g, g, zs[g])

        @pl.when(jnp.max(sum(off16)) > 0.0)
        def _():
            extract(True)
            zs_exact = pair_stage(True)[0]
            for g in range(HEADS_PER_ITER):
                finish(h0 + g, g, zs_exact[g])

        return carry

    lax.fori_loop(0, PEER_HEADS // HEADS_PER_ITER, iter_body, 0)


def _peer_select(x, wq_t, sk, tt=256):
    t, d = x.shape
    cflat, cvalid = _cand_tables()
    cflat = jnp.asarray(np.broadcast_to(cflat[:, None], (N_CAND, tt)))
    cvalid = jnp.asarray(np.broadcast_to(cvalid[:, None], (N_CAND, tt)))
    out = jax.ShapeDtypeStruct((PEER_HEADS, N_KEYS, t), F32)
    out16 = jax.ShapeDtypeStruct((PEER_HEADS, N_KEYS // BF16_TILE, BF16_TILE, t), BF16)
    ospec = pl.BlockSpec((PEER_HEADS, N_KEYS, tt), lambda i: (0, 0, i))
    ospec16 = pl.BlockSpec((PEER_HEADS, N_KEYS // BF16_TILE, BF16_TILE, tt), lambda i: (0, 0, 0, i))
    return pl.pallas_call(
        _select_kernel,
        out_shape=[out, out, out16, out16],
        grid=(t // tt,),
        in_specs=[pl.BlockSpec((tt, d), lambda i: (i, 0)),
                  pl.BlockSpec((2 * PEER_HEADS * 128, d), lambda i: (0, 0)),
                  pl.BlockSpec((2 * PEER_HEADS, N_KEYS, 128), lambda i: (0, 0, 0)),
                  pl.BlockSpec((N_CAND, tt), lambda i: (0, 0)),
                  pl.BlockSpec((N_CAND, tt), lambda i: (0, 0))],
        out_specs=[ospec, ospec, ospec16, ospec16],
        scratch_shapes=[pltpu.VMEM((2 * PEER_HEADS * 128, tt), BF16),
                        pltpu.VMEM((2 * HEADS_PER_ITER, N_KEYS, tt), F32),
                        pltpu.VMEM((2 * HEADS_PER_ITER, N_KEYS, tt), F32),
                        pltpu.VMEM((2 * HEADS_PER_ITER, N_KEYS, tt), F32),
                        pltpu.VMEM((2 * HEADS_PER_ITER, PEER_TOPK, tt), F32),
                        pltpu.VMEM((HEADS_PER_ITER, N_CAND, tt), F32),
                        pltpu.VMEM((HEADS_PER_ITER, N_CAND, tt), F32)],
        compiler_params=_cparams(("parallel",)),
        name="peer_select",
    )(x, wq_t, sk, cflat, cvalid)


SUB_E = 256


def _peer_kernel(x_ref, u_ref, vt_ref, p1_ref, n_ref, p2_ref, r2_ref, o_ref, xb_scr, w_scr, acc_scr):
    e = pl.program_id(1)
    ne = pl.num_programs(1)
    eb = u_ref.shape[0]
    tt = x_ref.shape[0]

    @pl.when(e == 0)
    def _():
        xb_scr[...] = x_ref[...].astype(BF16)
        acc_scr[...] = jnp.zeros(acc_scr.shape, F32)

    xb = xb_scr[...]
    inv_sqrt2 = 1.0 / math.sqrt(2.0)
    for sb in range(eb // SUB_E):
        hsub = lax.dot_general(u_ref[sb * SUB_E:(sb + 1) * SUB_E, :], xb, NT_DIMS,
                               preferred_element_type=F32)
        for ai in range(SUB_E // N_KEYS):
            a = sb * (SUB_E // N_KEYS) + ai
            hh = hsub[ai * N_KEYS:(ai + 1) * N_KEYS]
            gate = None
            for h in range(PEER_HEADS):
                nrow = jnp.broadcast_to(n_ref[h, a:a + 1, :], (BF16_TILE, tt)).astype(BF16)[None]
                prow = jnp.broadcast_to(p1_ref[h, a:a + 1, :], (BF16_TILE, tt)).astype(BF16)[None]
                term = jnp.where(r2_ref[h] < nrow, p2_ref[h], jnp.zeros((), BF16)) * prow
                gate = term if gate is None else gate + term
            act = hh * (1.0 + lax.erf(hh * inv_sqrt2))
            wgt = act.astype(BF16).reshape(N_KEYS // BF16_TILE, BF16_TILE, tt) * gate
            w_scr[a * N_KEYS:(a + 1) * N_KEYS, :] = wgt.reshape(N_KEYS, tt)
    acc_scr[...] += jnp.dot(vt_ref[...], w_scr[...], preferred_element_type=F32)

    @pl.when(e == ne - 1)
    def _():
        o_ref[...] = acc_scr[...]


def _peer_dense(x, u, vt, p1, nsel, p2, r2, tt=512, eb=2048):
    t, d = x.shape
    ra = eb // N_KEYS
    rows = pl.BlockSpec((PEER_HEADS, ra, tt), lambda i, e: (0, e, i))
    full = pl.BlockSpec((PEER_HEADS, N_KEYS // BF16_TILE, BF16_TILE, tt), lambda i, e: (0, 0, 0, i))
    return pl.pallas_call(
        _peer_kernel,
        out_shape=jax.ShapeDtypeStruct((d, t), F32),
        grid=(t // tt, N_EXPERTS // eb),
        in_specs=[pl.BlockSpec((tt, d), lambda i, e: (i, 0)),
                  pl.BlockSpec((eb, d), lambda i, e: (e, 0)),
                  pl.BlockSpec((d, eb), lambda i, e: (0, e)),
                  rows, rows, full, full],
        out_specs=pl.BlockSpec((d, tt), lambda i, e: (0, i)),
        scratch_shapes=[pltpu.VMEM((tt, d), BF16), pltpu.VMEM((eb, tt), BF16), pltpu.VMEM((d, tt), F32)],
        compiler_params=_cparams(("parallel", "arbitrary")),
        name="peer_dense",
    )(x, u, vt, p1, nsel, p2, r2)


def _final_kernel(x_ref, cht_ref, p_ref, wg_ref, wp_ref, g_ref, b_ref, o_ref):
    x = x_ref[...]
    gate = jax.nn.sigmoid(jnp.dot(x.astype(BF16), wg_ref[...], preferred_element_type=F32))
    ple = gate * jnp.dot(p_ref[...].astype(BF16), wp_ref[...], preferred_element_type=F32)
    ch = jnp.transpose(cht_ref[...], (1, 0))
    o_ref[...] = _layer_norm(ALPHA * x + ch + ple, g_ref[...], b_ref[...])


def _final(x, cht, p, wg, wp, g, b, tm=512):
    t, d = x.shape
    full = lambda r, c: pl.BlockSpec((r, c), lambda i: (0, 0))
    return pl.pallas_call(
        _final_kernel,
        out_shape=jax.ShapeDtypeStruct((t, d), F32),
        grid=(t // tm,),
        in_specs=[pl.BlockSpec((tm, d), lambda i: (i, 0)),
                  pl.BlockSpec((d, tm), lambda i: (0, i)),
                  pl.BlockSpec((tm, PLE_DIM), lambda i: (i, 0)),
                  full(d, d), full(PLE_DIM, d), full(1, d), full(1, d)],
        out_specs=pl.BlockSpec((tm, d), lambda i: (i, 0)),
        compiler_params=_cparams(("parallel",)),
        name="ple_ln2",
    )(x, cht, p, wg, wp, g.reshape(1, d), b.reshape(1, d))


def _diff_tile(s):
    return min(512, s)


def _trunk(x, p, prm, layers):
    b, s, d = x.shape
    t = b * s
    x = x.reshape(t, d)
    for i, lw in enumerate(layers):
        if i == 0:
            x, qa, ka, vat, qb, kb, vbt, ga, gb = _in_proj(x, lw["w_in"], lw["w_vbt"],
                                                           ln=(prm["emb_g"], prm["emb_b"]))
        else:
            qa, ka, vat, qb, kb, vbt, ga, gb = _in_proj(x, lw["w_in"], lw["w_vbt"])
        r3 = lambda m: m.reshape(b, s, m.shape[-1])
        a = _window_attn(r3(qa), r3(ka), vat, lw["bias_a"], lw["sink"])
        td = _diff_tile(s)
        bd = _diff_attn(r3(qb), r3(kb), vbt, lw["bias_b"][td], lw["lamv"], lw["norm_g"], lw["lam_init"], td)
        x = _attn_mix(a.reshape(t, 512), bd.reshape(t, 512), ga, gb, x,
                      lw["w_a"], lw["w_b"], lw["w_o"], lw["ln1_g"], lw["ln1_b"])
        p1, nsel, p2, r2 = _peer_select(x, lw["wq_t"], lw["subkeys"])
        cht = _peer_dense(x, lw["u"], lw["v_t"], p1, nsel, p2, r2)
        x = _final(x, cht, p[i].reshape(t, PLE_DIM), lw["ple_gate_w"], lw["ple_w"], lw["ln2_g"], lw["ln2_b"])
    return x.reshape(b, s, d)


def kernel(x_prompt, x_sample, p_prompt, p_sample, emb_ln_g, emb_ln_b, rel_bias, w_in, w_a, w_b, w_o, sink,
           lam_q1, lam_k1, lam_q2, lam_k2, diff_norm_g, ln1_g, ln1_b, peer_wq, peer_subkeys, peer_u, peer_v,
           ln2_g, ln2_b, ple_w, ple_gate_w):
    table_a = rel_bias[:, :HA]
    table_b = rel_bias[:, HA:]
    bias_a = _window_bias(table_a)
    diff_tiles = sorted({_diff_tile(x_prompt.shape[1]), _diff_tile(x_sample.shape[1])})
    bias_b = {t: _diff_bias(table_b, t) for t in diff_tiles}
    layers = []
    for i in range(DEPTH):
        layers.append(dict(
            w_in=_prep_w_in(w_in[i])[0],
            w_vbt=_prep_w_in(w_in[i])[1],
            bias_a=bias_a,
            bias_b=bias_b,
            sink=jnp.broadcast_to(sink[i].astype(F32)[:, None], (HA, 128)),
            lamv=jnp.stack([lam_q1[i], lam_k1[i], lam_q2[i], lam_k2[i]], axis=0).astype(F32),
            norm_g=diff_norm_g[i].astype(F32).reshape(1, DVB),
            lam_init=0.8 - 0.6 * math.exp(-0.3 * i),
            w_a=w_a[i].astype(BF16), w_b=w_b[i].astype(BF16), w_o=w_o[i].astype(BF16),
            ln1_g=ln1_g[i], ln1_b=ln1_b[i],
            wq_t=jnp.transpose(peer_wq[i]).astype(BF16),
            subkeys=peer_subkeys[i].reshape(2 * PEER_HEADS, N_KEYS, 128).astype(BF16),
            u=peer_u[i].astype(BF16),
            v_t=jnp.transpose(peer_v[i]).astype(BF16),
            ple_gate_w=ple_gate_w[i].astype(BF16), ple_w=ple_w[i].astype(BF16),
            ln2_g=ln2_g[i], ln2_b=ln2_b[i],
        ))
    prm = dict(emb_g=emb_ln_g, emb_b=emb_ln_b)
    y_prompt = _trunk(x_prompt, p_prompt, prm, layers)
    y_sample = _trunk(x_sample, p_sample, prm, layers)
    return (y_prompt, y_sample)
```
